```python
import math
import jax, jax.numpy as jnp
from jax import lax
import numpy as np

D_MODEL = 1024
BATCH = 8
SEQ = 2048
DEPTH = 1

CHUNK = 64
MEM_LEN = 256
EPS = 1e-6
S5_WIDTH = D_MODEL // 2
S5_GROUP_DIM = 16
S5_GROUPS = S5_WIDTH // S5_GROUP_DIM
S5_STATE = 64
SGU_WIDTH = D_MODEL // 2
SGU_GROUPS = 4
SGU_GROUP_DIM = SGU_WIDTH // SGU_GROUPS
SGU_BLOCK = 2 * CHUNK
XA_HEADS = 4
XA_HEAD_DIM = 128
XA_WIDTH = XA_HEADS * XA_HEAD_DIM
N_BRANCHES = 3
D_FF = -(-8 * D_MODEL // (3 * 256)) * 256
SPLITS = (S5_WIDTH, S5_WIDTH + SGU_WIDTH, S5_WIDTH + 2 * SGU_WIDTH,
          S5_WIDTH + 2 * SGU_WIDTH + XA_WIDTH)
IN_WIDTH = S5_WIDTH + 2 * SGU_WIDTH + XA_WIDTH + N_BRANCHES * D_MODEL

kernel_name = "hybrid_s5_gmlp_memxattn_block"


def _rms(x, g):
    xf = x.astype(jnp.float32)
    y = xf * lax.rsqrt(jnp.mean(xf * xf, axis=-1, keepdims=True) + EPS)
    return (y * g.astype(jnp.float32)).astype(x.dtype)


def _layernorm(x, g, b):
    xf = x.astype(jnp.float32)
    mu = jnp.mean(xf, axis=-1, keepdims=True)
    var = jnp.mean(jnp.square(xf - mu), axis=-1, keepdims=True)
    y = (xf - mu) * lax.rsqrt(var + EPS)
    return (y * g.astype(jnp.float32) + b.astype(jnp.float32)).astype(x.dtype)


def _s5_combine(e1, e2):
    a1r, a1i, b1r, b1i = e1
    a2r, a2i, b2r, b2i = e2
    ar = a1r * a2r - a1i * a2i
    ai = a1r * a2i + a1i * a2r
    br = a2r * b1r - a2i * b1i + b2r
    bi = a2r * b1i + a2i * b1r + b2i
    return ar, ai, br, bi


def _s5(u, a_re, a_im, log_dt, bm_re, bm_im, cm_re, cm_im, d_skip):
    bsz, seq, _ = u.shape
    uf = u.astype(jnp.float32)
    ug = uf.reshape(bsz, seq, S5_GROUPS, S5_GROUP_DIM)
    ar = a_re.astype(jnp.float32)
    ai = a_im.astype(jnp.float32)
    dt = jnp.exp(log_dt.astype(jnp.float32))[:, None]
    mag = jnp.exp(ar * dt)
    ab_re = mag * jnp.cos(ai * dt)
    ab_im = mag * jnp.sin(ai * dt)
    nr = ab_re - 1.0
    den = ar * ar + ai * ai
    f_re = (nr * ar + ab_im * ai) / den
    f_im = (ab_im * ar - nr * ai) / den
    br = bm_re.astype(jnp.float32)
    bi = bm_im.astype(jnp.float32)
    bb_re = f_re[..., None] * br - f_im[..., None] * bi
    bb_im = f_re[..., None] * bi + f_im[..., None] * br
    bu_re = jnp.einsum('blgh,gph->blgp', ug, bb_re)
    bu_im = jnp.einsum('blgh,gph->blgp', ug, bb_im)
    a_re_seq = jnp.broadcast_to(ab_re, (1, seq, S5_GROUPS, S5_STATE))
    a_im_seq = jnp.broadcast_to(ab_im, (1, seq, S5_GROUPS, S5_STATE))
    _, _, s_re, s_im = lax.associative_scan(
        _s5_combine, (a_re_seq, a_im_seq, bu_re, bu_im), axis=1)
    y = (jnp.einsum('blgp,ghp->blgh', s_re, cm_re.astype(jnp.float32))
         - jnp.einsum('blgp,ghp->blgh', s_im, cm_im.astype(jnp.float32)))
    y = y.reshape(bsz, seq, S5_WIDTH) + d_skip.astype(jnp.float32) * uf
    return y.astype(u.dtype)


def _sgu(zu, zv, g_sgu, beta_sgu, w_s, b_s):
    bsz, seq, _ = zu.shape
    u = jax.nn.gelu(zu)
    v = _layernorm(jax.nn.gelu(zv), g_sgu, beta_sgu)
    v = v.reshape(bsz, seq // SGU_BLOCK, SGU_BLOCK, SGU_GROUPS, SGU_GROUP_DIM)
    mask = jnp.tril(jnp.ones((SGU_BLOCK, SGU_BLOCK), dtype=w_s.dtype))
    w = w_s * mask[None]
    sv = jnp.einsum('gts,bcsgd->bctgd', w, v) + b_s.T[None, None, :, :, None]
    return u * sv.reshape(bsz, seq, SGU_WIDTH)


def _mem_xattn(zq, mem, g_mem, w_kv):
    bsz, seq, _ = zq.shape
    q = zq.reshape(bsz, seq, XA_HEADS, XA_HEAD_DIM)
    kv = _rms(mem, g_mem) @ w_kv
    k, v = jnp.split(kv, 2, axis=-1)
    k = k.reshape(bsz, MEM_LEN, XA_HEADS, XA_HEAD_DIM)
    v = v.reshape(bsz, MEM_LEN, XA_HEADS, XA_HEAD_DIM)
    s = jnp.einsum('blhd,bmhd->bhlm', q, k).astype(jnp.float32) * (XA_HEAD_DIM ** -0.5)
    p = jax.nn.softmax(s, axis=-1).astype(v.dtype)
    o = jnp.einsum('bhlm,bmhd->blhd', p, v)
    return o.reshape(bsz, seq, XA_WIDTH)


def setup_inputs(seed: int = 0) -> dict:
    key = jax.random.key(seed)
    ks = jax.random.split(key, 32)

    def nrm(k, shape, scale):
        return jax.random.normal(k, shape, jnp.float32) * scale

    def gain(k, shape):
        return 1.0 + 0.05 * jax.random.normal(k, shape, jnp.float32)

    L = DEPTH
    G, P, H = S5_GROUPS, S5_STATE, S5_GROUP_DIM
    a_re = -0.5 * jnp.exp(0.1 * jax.random.normal(ks[5], (L, G, P), jnp.float32))
    a_im = (math.pi * jnp.arange(P, dtype=jnp.float32))[None, None, :] + \
        0.1 * jax.random.normal(ks[6], (L, G, P), jnp.float32)
    log_dt = jax.random.uniform(ks[7], (L, G), jnp.float32,
                                minval=math.log(1e-3), maxval=math.log(1e-1))
    return {
        "x": jax.random.normal(ks[0], (BATCH, SEQ, D_MODEL), jnp.float32),
        "mem": jax.random.normal(ks[1], (BATCH, MEM_LEN, D_MODEL), jnp.float32),
        "g_mix": gain(ks[2], (L, D_MODEL)),
        "w_in": nrm(ks[3], (L, D_MODEL, IN_WIDTH), D_MODEL ** -0.5),
        "b_gate": nrm(ks[4], (L, N_BRANCHES * D_MODEL), 0.02),
        "a_re": a_re,
        "a_im": a_im,
        "log_dt": log_dt,
        "bm_re": nrm(ks[8], (L, G, P, H), (2 * H) ** -0.5),
        "bm_im": nrm(ks[9], (L, G, P, H), (2 * H) ** -0.5),
        "cm_re": nrm(ks[10], (L, G, H, P), (2 * P) ** -0.5),
        "cm_im": nrm(ks[11], (L, G, H, P), (2 * P) ** -0.5),
        "d_skip": nrm(ks[12], (L, S5_WIDTH), 1.0),
        "w_glu": nrm(ks[13], (L, S5_WIDTH, S5_WIDTH), S5_WIDTH ** -0.5),
        "b_glu": nrm(ks[14], (L, S5_WIDTH), 0.02),
        "g_sgu": gain(ks[15], (L, SGU_WIDTH)),
        "beta_sgu": nrm(ks[16], (L, SGU_WIDTH), 0.02),
        "w_s": nrm(ks[17], (L, SGU_GROUPS, SGU_BLOCK, SGU_BLOCK), SGU_BLOCK ** -0.5),
        "b_s": gain(ks[18], (L, SGU_GROUPS, SGU_BLOCK)),
        "g_mem": gain(ks[19], (L, D_MODEL)),
        "w_kv": nrm(ks[20], (L, D_MODEL, 2 * XA_WIDTH), D_MODEL ** -0.5),
        "w_a_out": nrm(ks[21], (L, S5_WIDTH, D_MODEL), S5_WIDTH ** -0.5),
        "w_b_out": nrm(ks[22], (L, SGU_WIDTH, D_MODEL), SGU_WIDTH ** -0.5),
        "w_c_out": nrm(ks[23], (L, XA_WIDTH, D_MODEL), XA_WIDTH ** -0.5),
        "w_out": nrm(ks[24], (L, D_MODEL, D_MODEL), D_MODEL ** -0.5),
        "g_ffn": gain(ks[25], (L, D_MODEL)),
        "w_gate": nrm(ks[26], (L, D_MODEL, D_FF), D_MODEL ** -0.5),
        "w_up": nrm(ks[27], (L, D_MODEL, D_FF), D_MODEL ** -0.5),
        "w_down": nrm(ks[28], (L, D_FF, D_MODEL), D_FF ** -0.5),
        "g_final": gain(ks[29], (D_MODEL,)),
    }


def reference(x, mem, g_mix, w_in, b_gate, a_re, a_im, log_dt, bm_re, bm_im, cm_re, cm_im,
              d_skip, w_glu, b_glu, g_sgu, beta_sgu, w_s, b_s, g_mem, w_kv,
              w_a_out, w_b_out, w_c_out, w_out, g_ffn, w_gate, w_up, w_down, g_final):
    bsz, seq, _ = x.shape
    for l in range(DEPTH):
        h = _rms(x, g_mix[l])
        z = h @ w_in[l]
        za, zu, zv, zq, zg = jnp.split(z, SPLITS, axis=-1)
        ya = jax.nn.gelu(_s5(za, a_re[l], a_im[l], log_dt[l], bm_re[l], bm_im[l],
                             cm_re[l], cm_im[l], d_skip[l]))
        ya = (ya * jax.nn.sigmoid(ya @ w_glu[l] + b_glu[l])) @ w_a_out[l]
        yb = _sgu(zu, zv, g_sgu[l], beta_sgu[l], w_s[l], b_s[l]) @ w_b_out[l]
        yc = _mem_xattn(zq, mem, g_mem[l], w_kv[l]) @ w_c_out[l]
        gates = jax.nn.sigmoid(zg + b_gate[l]).reshape(bsz, seq, N_BRANCHES, D_MODEL)
        merged = gates[:, :, 0] * ya + gates[:, :, 1] * yb + gates[:, :, 2] * yc
        x = x + merged @ w_out[l]
        hf = _rms(x, g_ffn[l])
        x = x + (jax.nn.silu(hf @ w_gate[l]) * (hf @ w_up[l])) @ w_down[l]
    return _rms(x, g_final)
```

```python
import functools

import jax
import jax.numpy as jnp
from jax import lax
from jax.experimental import pallas as pl
from jax.experimental.pallas import tpu as pltpu

F32 = jnp.float32
BF16 = jnp.bfloat16

EPS = 1e-6

S5_GROUP_DIM = 16
S5_STATE = 64
SGU_GROUPS = 4
SGU_BLOCK = 128
XA_HEADS = 4
XA_HEAD_DIM = 128
N_BRANCHES = 3

LANES = 128
SUBLANES = 8
MXU_DIM = 256
VMEM_LIMIT_BYTES = 58 * 1024 * 1024

TOKEN_TILE = 512
S5_TIME_CHUNK = 32
S5_SLAB_GROUPS = MXU_DIM // S5_GROUP_DIM
S5_PAIRS_PER_LOOP = 4
FFN_CHUNK = 256


def _dot(a, b):
    return jnp.dot(a, b, preferred_element_type=F32)


def _sigmoid(x):
    return 0.5 * jnp.tanh(0.5 * x) + 0.5


def _rms(xf, g):
    ms = jnp.mean(xf * xf, axis=-1, keepdims=True)
    return xf * lax.rsqrt(ms + EPS) * g


def _resident(shape):
    zeros = (0,) * len(shape)
    return pl.BlockSpec(shape, lambda *_: zeros, pipeline_mode=pl.Buffered(1))


def _params(n_axes):
    return pltpu.CompilerParams(
        dimension_semantics=("arbitrary",) * n_axes,
        vmem_limit_bytes=VMEM_LIMIT_BYTES,
    )


def _mem_kv_kernel(mem_ref, g_ref, w_ref, k_ref, v_ref):
    width = k_ref.shape[-1]
    h = _rms(mem_ref[0], g_ref[...]).astype(BF16)
    k_ref[0] = _dot(h, w_ref[:, :width]).astype(BF16)
    v_ref[0] = _dot(h, w_ref[:, width:]).astype(BF16)


def _mem_kv(mem, g_mem, w_kv):
    bsz, mlen, d = mem.shape
    width = w_kv.shape[1] // 2
    out = jax.ShapeDtypeStruct((bsz, mlen, width), BF16)
    return pl.pallas_call(
        _mem_kv_kernel,
        out_shape=(out, out),
        grid=(bsz,),
        in_specs=[
            pl.BlockSpec((1, mlen, d), lambda b: (b, 0, 0)),
            _resident((1, d)),
            _resident(w_kv.shape),
        ],
        out_specs=(
            pl.BlockSpec((1, mlen, width), lambda b: (b, 0, 0)),
            pl.BlockSpec((1, mlen, width), lambda b: (b, 0, 0)),
        ),
        compiler_params=_params(1),
        name="mem_kv",
    )(mem, g_mem, w_kv)


def _inproj_kernel(x_ref, gmix_ref, win_ref, bgate_ref, gsgu_ref, betasgu_ref, ws_ref, bst_ref,
                   k_ref, v_ref, za_ref, sgu_ref, attn_ref, gates_ref, h_scr, u_scr, vn_scr):
    tt = x_ref.shape[1]
    width = za_ref.shape[-1]

    h_scr[...] = _rms(x_ref[0], gmix_ref[...]).astype(BF16)
    h = h_scr[...]

    def proj(idx):
        return _dot(h, win_ref[:, idx * width:(idx + 1) * width])

    za_ref[0] = proj(0)

    u_scr[...] = jax.nn.gelu(proj(1))
    gv = jax.nn.gelu(proj(2))
    mu = jnp.mean(gv, axis=-1, keepdims=True)
    cen = gv - mu
    var = jnp.mean(cen * cen, axis=-1, keepdims=True)
    vn_scr[...] = (cen * lax.rsqrt(var + EPS) * gsgu_ref[...] + betasgu_ref[...]).astype(BF16)

    row = lax.broadcasted_iota(jnp.int32, (SGU_BLOCK, SGU_BLOCK), 0)
    col = lax.broadcasted_iota(jnp.int32, (SGU_BLOCK, SGU_BLOCK), 1)
    tril = (row >= col).astype(F32)
    gdim = width // SGU_GROUPS
    for g in range(SGU_GROUPS):
        w_g = (ws_ref[g] * tril).astype(BF16)
        bias = bst_ref[:, g:g + 1]
        lanes = slice(g * gdim, (g + 1) * gdim)
        for r in range(tt // SGU_BLOCK):
            rows = slice(r * SGU_BLOCK, (r + 1) * SGU_BLOCK)
            sv = _dot(w_g, vn_scr[rows, lanes]) + bias
            sgu_ref[0, rows, lanes] = (u_scr[rows, lanes] * sv).astype(BF16)

    q = proj(3).astype(BF16)
    scale = XA_HEAD_DIM ** -0.5
    for hd in range(XA_HEADS):
        lanes = slice(hd * XA_HEAD_DIM, (hd + 1) * XA_HEAD_DIM)
        s = lax.dot_general(q[:, lanes], k_ref[0, :, lanes], (((1,), (1,)), ((), ())),
                            preferred_element_type=F32) * scale
        p = jnp.exp(s - jnp.max(s, axis=-1, keepdims=True))
        denom = jnp.sum(p, axis=-1, keepdims=True)
        o = _dot(p.astype(BF16), v_ref[0, :, lanes])
        attn_ref[0, :, lanes] = (o / denom).astype(BF16)

    n_gate_chunks = gates_ref.shape[-1] // width
    for c in range(n_gate_chunks):
        lanes = slice(c * width, (c + 1) * width)
        zg = proj(4 + c) + bgate_ref[:, lanes]
        gates_ref[0, :, lanes] = _sigmoid(zg).astype(BF16)


def _inproj(x, g_mix, w_in, b_gate, g_sgu, beta_sgu, w_s, b_s_t, k, v):
    bsz, seq, d = x.shape
    width = g_sgu.shape[-1]
    gate_w = b_gate.shape[-1]
    mlen = k.shape[1]
    tt = TOKEN_TILE
    tile = lambda w: pl.BlockSpec((1, tt, w), lambda b, i: (b, i, 0))
    per_batch = pl.BlockSpec((1, mlen, width), lambda b, i: (b, 0, 0))
    return pl.pallas_call(
        _inproj_kernel,
        out_shape=(
            jax.ShapeDtypeStruct((bsz, seq, width), F32),
            jax.ShapeDtypeStruct((bsz, seq, width), BF16),
            jax.ShapeDtypeStruct((bsz, seq, width), BF16),
            jax.ShapeDtypeStruct((bsz, seq, gate_w), BF16),
        ),
        grid=(bsz, seq // tt),
        in_specs=[
            tile(d),
            _resident((1, d)),
            _resident(w_in.shape),
            _resident((1, gate_w)),
            _resident((1, width)),
            _resident((1, width)),
            _resident(w_s.shape),
            _resident(b_s_t.shape),
            per_batch,
            per_batch,
        ],
        out_specs=(tile(width), tile(width), tile(width), tile(gate_w)),
        scratch_shapes=[
            pltpu.VMEM((tt, d), BF16),
            pltpu.VMEM((tt, width), F32),
            pltpu.VMEM((tt, width), BF16),
        ],
        compiler_params=_params(2),
        name="inproj",
    )(x, g_mix, w_in, b_gate, g_sgu, beta_sgu, w_s, b_s_t, k, v)


def _s5_kernel(za_ref, wb_ref, wc_ref, are_ref, aim_ref, dskip_ref, out_ref,
               zat_scr, bu_scr, yt_scr, st_scr):
    bsz, tc, width = za_ref.shape
    n_slabs, _, slab_cols = wb_ref.shape
    half = slab_cols // 2
    n_lane_blocks = width // LANES

    @pl.when(pl.program_id(0) == 0)
    def _():
        st_scr[...] = jnp.zeros_like(st_scr)

    for b in range(bsz):
        for j in range(n_lane_blocks):
            zat_scr[j, pl.ds(b, tc, stride=bsz), :] = za_ref[b, :, j * LANES:(j + 1) * LANES]

    blocks_per_slab = n_lane_blocks // n_slabs
    for j in range(n_slabs):
        lhs = jnp.concatenate(
            [zat_scr[j * blocks_per_slab + q] for q in range(blocks_per_slab)], axis=-1)
        bu_scr[:, j * slab_cols:(j + 1) * slab_cols] = _dot(lhs.astype(BF16), wb_ref[j])

    pairs = [(j, kk) for j in range(n_slabs) for kk in range(half // LANES)]
    for p0 in range(0, len(pairs), S5_PAIRS_PER_LOOP):
        group = pairs[p0:p0 + S5_PAIRS_PER_LOOP]
        cols = [(j * slab_cols + kk * LANES, j * slab_cols + half + kk * LANES) for j, kk in group]
        a_cols = [j * half + kk * LANES for j, kk in group]
        a_re = [jnp.broadcast_to(are_ref[:, c:c + LANES], (bsz, LANES)) for c in a_cols]
        a_im = [jnp.broadcast_to(aim_ref[:, c:c + LANES], (bsz, LANES)) for c in a_cols]
        init = []
        for cre, cim in cols:
            init += [st_scr[:, cre:cre + LANES], st_scr[:, cim:cim + LANES]]

        def step(t, carry, cols=cols, a_re=a_re, a_im=a_im):
            rows = pl.ds(pl.multiple_of(t * bsz, bsz), bsz)
            new = []
            for idx, (cre, cim) in enumerate(cols):
                sr, si = carry[2 * idx], carry[2 * idx + 1]
                nsr = a_re[idx] * sr - a_im[idx] * si + bu_scr[rows, cre:cre + LANES]
                nsi = a_re[idx] * si + a_im[idx] * sr + bu_scr[rows, cim:cim + LANES]
                bu_scr[rows, cre:cre + LANES] = nsr
                bu_scr[rows, cim:cim + LANES] = nsi
                new += [nsr, nsi]
            return tuple(new)

        final = lax.fori_loop(0, tc, step, tuple(init), unroll=8)
        for idx, (cre, cim) in enumerate(cols):
            st_scr[:, cre:cre + LANES] = final[2 * idx]
            st_scr[:, cim:cim + LANES] = final[2 * idx + 1]

    for j in range(n_slabs):
        y = _dot(bu_scr[:, j * slab_cols:(j + 1) * slab_cols].astype(BF16), wc_ref[j])
        for q in range(blocks_per_slab):
            yt_scr[j * blocks_per_slab + q] = y[:, q * LANES:(q + 1) * LANES]

    for b in range(bsz):
        for j in range(n_lane_blocks):
            lanes = slice(j * LANES, (j + 1) * LANES)
            y = yt_scr[j, pl.ds(b, tc, stride=bsz), :]
            out_ref[b, :, lanes] = jax.nn.gelu(y + dskip_ref[:, lanes] * za_ref[b, :, lanes])


def _s5_scan(za, wb, wc, a_re_row, a_im_row, d_skip):
    bsz, seq, width = za.shape
    assert bsz == SUBLANES, "one time step of all batches must fill the sublanes of a vreg"
    tc = S5_TIME_CHUNK
    state_cols = wb.shape[0] * wb.shape[2]
    blk = pl.BlockSpec((bsz, tc, width), lambda i: (0, i, 0))
    return pl.pallas_call(
        _s5_kernel,
        out_shape=jax.ShapeDtypeStruct((bsz, seq, width), F32),
        grid=(seq // tc,),
        in_specs=[
            blk,
            _resident(wb.shape),
            _resident(wc.shape),
            _resident(a_re_row.shape),
            _resident(a_im_row.shape),
            _resident(d_skip.shape),
        ],
        out_specs=blk,
        scratch_shapes=[
            pltpu.VMEM((width // LANES, tc * bsz, LANES), F32),
            pltpu.VMEM((tc * bsz, state_cols), F32),
            pltpu.VMEM((width // LANES, tc * bsz, LANES), F32),
            pltpu.VMEM((bsz, state_cols), F32),
        ],
        compiler_params=_params(1),
        name="s5_scan",
    )(za, wb, wc, a_re_row, a_im_row, d_skip)


def _s5_weights(a_re, a_im, log_dt, bm_re, bm_im, cm_re, cm_im):
    n_groups, n_state = a_re.shape
    n_slabs = n_groups // S5_SLAB_GROUPS
    dt = jnp.exp(log_dt)[:, None]
    mag = jnp.exp(a_re * dt)
    ab_re = mag * jnp.cos(a_im * dt)
    ab_im = mag * jnp.sin(a_im * dt)
    nr = ab_re - 1.0
    den = a_re * a_re + a_im * a_im
    f_re = (nr * a_re + ab_im * a_im) / den
    f_im = (ab_im * a_re - nr * a_im) / den
    bb_re = f_re[..., None] * bm_re - f_im[..., None] * bm_im
    bb_im = f_re[..., None] * bm_im + f_im[..., None] * bm_re
    eye = jnp.eye(S5_SLAB_GROUPS, dtype=F32)
    slab_in = S5_SLAB_GROUPS * S5_GROUP_DIM
    slab_state = S5_SLAB_GROUPS * n_state

    def in_slab(w):
        w = w.reshape(n_slabs, S5_SLAB_GROUPS, n_state, S5_GROUP_DIM)
        return jnp.einsum("jgph,gk->jghkp", w, eye).reshape(n_slabs, slab_in, slab_state)

    def out_slab(w):
        w = w.reshape(n_slabs, S5_SLAB_GROUPS, S5_GROUP_DIM, n_state)
        return jnp.einsum("jghp,gk->jgpkh", w, eye).reshape(n_slabs, slab_state, slab_in)

    wb = jnp.concatenate([in_slab(bb_re), in_slab(bb_im)], axis=-1).astype(BF16)
    wc = jnp.concatenate([out_slab(cm_re), -out_slab(cm_im)], axis=1).astype(BF16)
    return wb, wc, ab_re.reshape(1, -1), ab_im.reshape(1, -1)


def _merge_ffn_kernel(x_ref, yp_ref, sgu_ref, attn_ref, gates_ref, wglu_ref, bglu_ref, wa_ref,
                      wb_ref, wc_ref, wout_ref, gffn_ref, wgate_ref, wup_ref, wdown_ref, gfin_ref,
                      out_ref, mrg_scr, x1_scr, hf_scr, act_scr, *, final_norm):
    d = x_ref.shape[-1]
    width = yp_ref.shape[-1]
    d_ff = wdown_ref.shape[0]

    yp = yp_ref[...]
    glu = (yp * _sigmoid(_dot(yp.astype(BF16), wglu_ref[...]) + bglu_ref[...])).astype(BF16)
    for c in range(d // width):
        lanes = slice(c * width, (c + 1) * width)
        ya = _dot(glu, wa_ref[:, lanes])
        yb = _dot(sgu_ref[...], wb_ref[:, lanes])
        yc = _dot(attn_ref[...], wc_ref[:, lanes])
        g0 = gates_ref[:, c * width:(c + 1) * width].astype(F32)
        g1 = gates_ref[:, d + c * width:d + (c + 1) * width].astype(F32)
        g2 = gates_ref[:, 2 * d + c * width:2 * d + (c + 1) * width].astype(F32)
        mrg_scr[:, lanes] = (g0 * ya + g1 * yb + g2 * yc).astype(BF16)
    x1_scr[...] = x_ref[...] + _dot(mrg_scr[...], wout_ref[...])

    hf_scr[...] = _rms(x1_scr[...], gffn_ref[...]).astype(BF16)
    hf = hf_scr[...]
    for c in range(d_ff // FFN_CHUNK):
        cols = slice(c * FFN_CHUNK, (c + 1) * FFN_CHUNK)
        gate = _dot(hf, wgate_ref[:, cols])
        up = _dot(hf, wup_ref[:, cols])
        act_scr[:, cols] = (gate * _sigmoid(gate) * up).astype(BF16)
    x2 = x1_scr[...] + _dot(act_scr[...], wdown_ref[...])
    out_ref[...] = _rms(x2, gfin_ref[...]) if final_norm else x2


def _merge_ffn(x, yp, sgu, attn, gates, w_glu, b_glu, w_a, w_b, w_c, w_out, g_ffn,
               w_gate, w_up, w_down, g_final, final_norm):
    n_tok, d = x.shape
    width = yp.shape[-1]
    d_ff = w_down.shape[0]
    tm = TOKEN_TILE
    tile = lambda w: pl.BlockSpec((tm, w), lambda i: (i, 0))
    return pl.pallas_call(
        functools.partial(_merge_ffn_kernel, final_norm=final_norm),
        out_shape=jax.ShapeDtypeStruct((n_tok, d), F32),
        grid=(n_tok // tm,),
        in_specs=[
            tile(d), tile(width), tile(width), tile(width), tile(gates.shape[-1]),
            _resident(w_glu.shape), _resident((1, width)),
            _resident(w_a.shape), _resident(w_b.shape), _resident(w_c.shape),
            _resident(w_out.shape), _resident((1, d)),
            _resident(w_gate.shape), _resident(w_up.shape), _resident(w_down.shape),
            _resident((1, d)),
        ],
        out_specs=tile(d),
        scratch_shapes=[
            pltpu.VMEM((tm, d), BF16),
            pltpu.VMEM((tm, d), F32),
            pltpu.VMEM((tm, d), BF16),
            pltpu.VMEM((tm, d_ff), BF16),
        ],
        compiler_params=_params(1),
        name="merge_ffn",
    )(x, yp, sgu, attn, gates, w_glu, b_glu, w_a, w_b, w_c, w_out, g_ffn,
      w_gate, w_up, w_down, g_final)


def kernel(x, mem, g_mix, w_in, b_gate, a_re, a_im, log_dt, bm_re, bm_im, cm_re, cm_im, d_skip, w_glu, b_glu, g_sgu, beta_sgu, w_s, b_s, g_mem, w_kv, w_a_out, w_b_out, w_c_out, w_out, g_ffn, w_gate, w_up, w_down, g_final):
    bsz, seq, d = x.shape
    depth = w_in.shape[0]
    row = lambda v: v.reshape(1, -1)
    for l in range(depth):
        k, v = _mem_kv(mem, row(g_mem[l]), w_kv[l].astype(BF16))
        za, sgu, attn, gates = _inproj(
            x, row(g_mix[l]), w_in[l].astype(BF16), row(b_gate[l]), row(g_sgu[l]),
            row(beta_sgu[l]), w_s[l], b_s[l].T, k, v)
        wb, wc, a_re_row, a_im_row = _s5_weights(
            a_re[l], a_im[l], log_dt[l], bm_re[l], bm_im[l], cm_re[l], cm_im[l])
        yp = _s5_scan(za, wb, wc, a_re_row, a_im_row, row(d_skip[l]))
        flat = lambda t: t.reshape(bsz * seq, t.shape[-1])
        x = _merge_ffn(
            flat(x), flat(yp), flat(sgu), flat(attn), flat(gates),
            w_glu[l].astype(BF16), row(b_glu[l]), w_a_out[l].astype(BF16),
            w_b_out[l].astype(BF16), w_c_out[l].astype(BF16), w_out[l].astype(BF16),
            row(g_ffn[l]), w_gate[l].astype(BF16), w_up[l].astype(BF16),
            w_down[l].astype(BF16), row(g_final), l == depth - 1).reshape(bsz, seq, d)
    return x
```

```python
import functools

import jax
import jax.numpy as jnp
from jax import lax
from jax.experimental import pallas as pl
from jax.experimental.pallas import tpu as pltpu

F32 = jnp.float32
BF16 = jnp.bfloat16

EPS = 1e-6

S5_GROUP_DIM = 16
SGU_GROUPS = 4
SGU_BLOCK = 128
XA_HEADS = 4
XA_HEAD_DIM = 128

LANES = 128
SUBLANES = 8
MXU_DIM = 256
VMEM_LIMIT_BYTES = 58 * 1024 * 1024

TOKEN_TILE = 512
S5_CHUNK = MXU_DIM // S5_GROUP_DIM
S5_LANE_GROUPS = LANES // S5_GROUP_DIM
FFN_CHUNK = 256


def _dot(a, b):
    return jnp.dot(a, b, preferred_element_type=F32)


def _sigmoid(x):
    return 0.5 * jnp.tanh(0.5 * x) + 0.5


def _rms(xf, g):
    ms = jnp.mean(xf * xf, axis=-1, keepdims=True)
    return xf * lax.rsqrt(ms + EPS) * g


def _resident(shape):
    zeros = (0,) * len(shape)
    return pl.BlockSpec(shape, lambda *_: zeros, pipeline_mode=pl.Buffered(1))


def _params(n_axes):
    return pltpu.CompilerParams(
        dimension_semantics=("arbitrary",) * n_axes,
        vmem_limit_bytes=VMEM_LIMIT_BYTES,
    )


def _butterfly(vs, index, period, axis):
    vs = list(vs)
    size = vs[0].shape[axis]
    for d in (4, 2, 1):
        keep = (index & d) == 0
        for v in range(8):
            if v & d == 0:
                a, b = vs[v], vs[v + d]
                vs[v] = jnp.where(keep, a, pltpu.roll(b, d * period, axis=axis))
                vs[v + d] = jnp.where(keep, pltpu.roll(a, size - d * period, axis=axis), b)
    return vs


def _swap_list_sublane(vs):
    axis = vs[0].ndim - 2
    return _butterfly(vs, lax.broadcasted_iota(jnp.int32, vs[0].shape, axis), 1, axis)


def _swap_list_laneblock(vs):
    axis = vs[0].ndim - 1
    lane = lax.broadcasted_iota(jnp.int32, vs[0].shape, axis)
    return _butterfly(vs, lane // S5_GROUP_DIM, S5_GROUP_DIM, axis)


def _mem_kv_kernel(mem_ref, g_ref, w_ref, k_ref, v_ref):
    width = k_ref.shape[-1]
    h = _rms(mem_ref[0], g_ref[...]).astype(BF16)
    k_ref[0] = _dot(h, w_ref[:, :width]).astype(BF16)
    v_ref[0] = _dot(h, w_ref[:, width:]).astype(BF16)


def _mem_kv(mem, g_mem, w_kv):
    bsz, mlen, d = mem.shape
    width = w_kv.shape[1] // 2
    out = jax.ShapeDtypeStruct((bsz, mlen, width), BF16)
    return pl.pallas_call(
        _mem_kv_kernel,
        out_shape=(out, out),
        grid=(bsz,),
        in_specs=[
            pl.BlockSpec((1, mlen, d), lambda b: (b, 0, 0)),
            _resident((1, d)),
            _resident(w_kv.shape),
        ],
        out_specs=(
            pl.BlockSpec((1, mlen, width), lambda b: (b, 0, 0)),
            pl.BlockSpec((1, mlen, width), lambda b: (b, 0, 0)),
        ),
        compiler_params=_params(1),
        name="mem_kv",
    )(mem, g_mem, w_kv)


def _inproj_kernel(x_ref, gmix_ref, win_ref, bgate_ref, gsgu_ref, betasgu_ref, ws_ref, bst_ref,
                   k_ref, v_ref, za_ref, uvec_ref, sgu_ref, attn_ref, gates_ref,
                   h_scr, u_scr, vn_scr):
    tt = x_ref.shape[1]
    width = za_ref.shape[-1]

    h_scr[...] = _rms(x_ref[0], gmix_ref[...]).astype(BF16)
    h = h_scr[...]

    def proj(idx):
        return _dot(h, win_ref[:, idx * width:(idx + 1) * width])

    za = proj(0)
    za_ref[0] = za
    supers = tt // (SUBLANES * S5_CHUNK)
    za5 = za.reshape(supers, SUBLANES, S5_CHUNK // SUBLANES, SUBLANES, width)
    for lb in range(width // LANES):
        lanes = slice(lb * LANES, (lb + 1) * LANES)
        vs = [za5[:, c, :, :, lanes] for c in range(SUBLANES)]
        vs = _swap_list_sublane(vs)
        vs = _swap_list_laneblock(vs)
        for gi in range(S5_LANE_GROUPS):
            w = vs[gi]
            rows = jnp.concatenate([w[:, hf] for hf in range(S5_CHUNK // SUBLANES)], axis=-1)
            uvec_ref[lb * S5_LANE_GROUPS + gi] = rows.reshape(supers * SUBLANES, -1).astype(BF16)

    u_scr[...] = jax.nn.gelu(proj(1))
    gv = jax.nn.gelu(proj(2))
    mu = jnp.mean(gv, axis=-1, keepdims=True)
    cen = gv - mu
    var = jnp.mean(cen * cen, axis=-1, keepdims=True)
    vn_scr[...] = (cen * lax.rsqrt(var + EPS) * gsgu_ref[...] + betasgu_ref[...]).astype(BF16)

    row = lax.broadcasted_iota(jnp.int32, (SGU_BLOCK, SGU_BLOCK), 0)
    col = lax.broadcasted_iota(jnp.int32, (SGU_BLOCK, SGU_BLOCK), 1)
    tril = (row >= col).astype(F32)
    gdim = width // SGU_GROUPS
    for g in range(SGU_GROUPS):
        w_g = (ws_ref[g] * tril).astype(BF16)
        bias = bst_ref[:, g:g + 1]
        lanes = slice(g * gdim, (g + 1) * gdim)
        for r in range(tt // SGU_BLOCK):
            rows = slice(r * SGU_BLOCK, (r + 1) * SGU_BLOCK)
            sv = _dot(w_g, vn_scr[rows, lanes]) + bias
            sgu_ref[0, rows, lanes] = (u_scr[rows, lanes] * sv).astype(BF16)

    q = proj(3).astype(BF16)
    scale = XA_HEAD_DIM ** -0.5
    for hd in range(XA_HEADS):
        lanes = slice(hd * XA_HEAD_DIM, (hd + 1) * XA_HEAD_DIM)
        s = lax.dot_general(q[:, lanes], k_ref[0, :, lanes], (((1,), (1,)), ((), ())),
                            preferred_element_type=F32) * scale
        p = jnp.exp(s - jnp.max(s, axis=-1, keepdims=True))
        denom = jnp.sum(p, axis=-1, keepdims=True)
        o = _dot(p.astype(BF16), v_ref[0, :, lanes])
        attn_ref[0, :, lanes] = (o / denom).astype(BF16)

    n_gate_chunks = gates_ref.shape[-1] // width
    for c in range(n_gate_chunks):
        lanes = slice(c * width, (c + 1) * width)
        zg = proj(4 + c) + bgate_ref[:, lanes]
        gates_ref[0, :, lanes] = _sigmoid(zg).astype(BF16)


def _inproj(x, g_mix, w_in, b_gate, g_sgu, beta_sgu, w_s, b_s_t, k, v):
    bsz, seq, d = x.shape
    width = g_sgu.shape[-1]
    gate_w = b_gate.shape[-1]
    mlen = k.shape[1]
    tt = TOKEN_TILE
    n_groups = width // S5_GROUP_DIM
    tiles = seq // tt
    chunk_rows = tt // S5_CHUNK
    tile = lambda w: pl.BlockSpec((1, tt, w), lambda b, i: (b, i, 0))
    per_batch = pl.BlockSpec((1, mlen, width), lambda b, i: (b, 0, 0))
    return pl.pallas_call(
        _inproj_kernel,
        out_shape=(
            jax.ShapeDtypeStruct((bsz, seq, width), F32),
            jax.ShapeDtypeStruct((n_groups, bsz * seq // S5_CHUNK, MXU_DIM), BF16),
            jax.ShapeDtypeStruct((bsz, seq, width), BF16),
            jax.ShapeDtypeStruct((bsz, seq, width), BF16),
            jax.ShapeDtypeStruct((bsz, seq, gate_w), BF16),
        ),
        grid=(bsz, tiles),
        in_specs=[
            tile(d),
            _resident((1, d)),
            _resident(w_in.shape),
            _resident((1, gate_w)),
            _resident((1, width)),
            _resident((1, width)),
            _resident(w_s.shape),
            _resident(b_s_t.shape),
            per_batch,
            per_batch,
        ],
        out_specs=(
            tile(width),
            pl.BlockSpec((n_groups, chunk_rows, MXU_DIM), lambda b, i: (0, b * tiles + i, 0)),
            tile(width), tile(width), tile(gate_w),
        ),
        scratch_shapes=[
            pltpu.VMEM((tt, d), BF16),
            pltpu.VMEM((tt, width), F32),
            pltpu.VMEM((tt, width), BF16),
        ],
        compiler_params=_params(2),
        name="inproj",
    )(x, g_mix, w_in, b_gate, g_sgu, beta_sgu, w_s, b_s_t, k, v)


def _s5_chunk_kernel(u_ref, wconv_ref, wus_ref, wsy_ref, are_ref, aim_ref, y_ref, et_scr, st_scr):
    n_rows = u_ref.shape[1]
    n_chunks = n_rows // SUBLANES
    cols = wus_ref.shape[-1]
    half = cols // 2
    u0, u1 = u_ref[0], u_ref[1]

    e = _dot(jnp.concatenate([u0, u1], axis=-1), wus_ref[0])
    e4 = e.reshape(SUBLANES, n_chunks // SUBLANES, SUBLANES, cols)
    vs = _swap_list_sublane([e4[b] for b in range(SUBLANES)])
    et_scr[...] = jnp.stack(vs, axis=1).reshape(n_chunks, SUBLANES, cols)

    a_re = jnp.broadcast_to(are_ref[0], (SUBLANES, half))
    a_im = jnp.broadcast_to(aim_ref[0], (SUBLANES, half))

    def step(c, carry):
        s_re, s_im = carry
        st_scr[c] = jnp.concatenate([s_re, s_im], axis=-1)
        e_c = et_scr[c]
        return (a_re * s_re - a_im * s_im + e_c[:, :half],
                a_re * s_im + a_im * s_re + e_c[:, half:])

    zero = jnp.zeros((SUBLANES, half), F32)
    lax.fori_loop(0, n_chunks, step, (zero, zero), unroll=8)

    s4 = st_scr[...].reshape(n_chunks // SUBLANES, SUBLANES, SUBLANES, cols)
    ws = _swap_list_sublane([s4[:, s] for s in range(SUBLANES)])
    s_rows = jnp.stack(ws, axis=0).reshape(n_rows, cols).astype(BF16)
    y_state = _dot(s_rows, wsy_ref[0])
    out_w = y_ref.shape[-1]
    y_ref[0] = _dot(u0, wconv_ref[0]) + y_state[:, :out_w]
    y_ref[1] = _dot(u1, wconv_ref[1]) + y_state[:, out_w:]


def _s5_chunk(uvec, w_conv, w_us, w_sy, a_re, a_im):
    n_groups, n_rows, lanes = uvec.shape
    pairs = n_groups // 2
    grp = lambda shape: pl.BlockSpec(shape, lambda q: (q, 0, 0))
    return pl.pallas_call(
        _s5_chunk_kernel,
        out_shape=jax.ShapeDtypeStruct((n_groups, n_rows, lanes), F32),
        grid=(pairs,),
        in_specs=[
            grp((2, n_rows, lanes)),
            grp((2,) + w_conv.shape[1:]),
            grp((1,) + w_us.shape[1:]),
            grp((1,) + w_sy.shape[1:]),
            grp((1,) + a_re.shape[1:]),
            grp((1,) + a_im.shape[1:]),
        ],
        out_specs=grp((2, n_rows, lanes)),
        scratch_shapes=[
            pltpu.VMEM((n_rows // SUBLANES, SUBLANES, w_us.shape[-1]), F32),
            pltpu.VMEM((n_rows // SUBLANES, SUBLANES, w_us.shape[-1]), F32),
        ],
        compiler_params=_params(1),
        name="s5_chunk",
    )(uvec, w_conv, w_us, w_sy, a_re, a_im)


def _s5_chunk_weights(a_re, a_im, log_dt, bm_re, bm_im, cm_re, cm_im):
    n_groups, n_state = a_re.shape
    tc, gd = S5_CHUNK, S5_GROUP_DIM
    hp = lax.Precision.HIGHEST
    dt = jnp.exp(log_dt)[:, None]
    mag = jnp.exp(a_re * dt)
    ab_re = mag * jnp.cos(a_im * dt)
    ab_im = mag * jnp.sin(a_im * dt)
    nr = ab_re - 1.0
    den = a_re * a_re + a_im * a_im
    f_re = (nr * a_re + ab_im * a_im) / den
    f_im = (ab_im * a_re - nr * a_im) / den
    bb_re = f_re[..., None] * bm_re - f_im[..., None] * bm_im
    bb_im = f_re[..., None] * bm_im + f_im[..., None] * bm_re

    n = jnp.arange(tc + 1, dtype=F32)[:, None, None]
    pw_mag = jnp.exp(n * (a_re * dt))
    pw_re = pw_mag * jnp.cos(n * (a_im * dt))
    pw_im = pw_mag * jnp.sin(n * (a_im * dt))
    ca_re = cm_re[None] * pw_re[:, :, None, :] - cm_im[None] * pw_im[:, :, None, :]
    ca_im = cm_re[None] * pw_im[:, :, None, :] + cm_im[None] * pw_re[:, :, None, :]

    kern = (jnp.einsum("nghp,gpk->nghk", ca_re[:tc], bb_re, precision=hp)
            - jnp.einsum("nghp,gpk->nghk", ca_im[:tc], bb_im, precision=hp))
    lag = jnp.arange(tc)[None, :] - jnp.arange(tc)[:, None]
    toe = jnp.where((lag >= 0)[:, :, None, None, None], kern[jnp.maximum(lag, 0)], 0.0)
    w_conv = toe.transpose(2, 0, 4, 1, 3).reshape(n_groups, tc * gd, tc * gd)

    pr, pi = pw_re[:tc][::-1], pw_im[:tc][::-1]
    us_re = pr[..., None] * bb_re[None] - pi[..., None] * bb_im[None]
    us_im = pr[..., None] * bb_im[None] + pi[..., None] * bb_re[None]
    to_rows = lambda w: w.transpose(1, 0, 3, 2).reshape(n_groups, tc * gd, n_state)
    us_re, us_im = to_rows(us_re), to_rows(us_im)
    to_cols = lambda w: w.transpose(1, 3, 0, 2).reshape(n_groups, n_state, tc * gd)
    sy_re, sy_im = to_cols(ca_re[1:]), to_cols(-ca_im[1:])

    z_us = jnp.zeros_like(us_re[0::2])
    w_us = jnp.concatenate([
        jnp.concatenate([us_re[0::2], z_us, us_im[0::2], z_us], axis=-1),
        jnp.concatenate([z_us, us_re[1::2], z_us, us_im[1::2]], axis=-1)], axis=1)
    z_sy = jnp.zeros_like(sy_re[0::2])
    w_sy = jnp.concatenate([
        jnp.concatenate([sy_re[0::2], z_sy], axis=-1),
        jnp.concatenate([z_sy, sy_re[1::2]], axis=-1),
        jnp.concatenate([sy_im[0::2], z_sy], axis=-1),
        jnp.concatenate([z_sy, sy_im[1::2]], axis=-1)], axis=1)
    pair = lambda w: jnp.concatenate([w[0::2], w[1::2]], axis=-1)[:, None, :]
    return (w_conv.astype(BF16), w_us.astype(BF16), w_sy.astype(BF16),
            pair(pw_re[tc]), pair(pw_im[tc]))


def _merge_ffn_kernel(x_ref, yvec_ref, za_ref, dskip_ref, sgu_ref, attn_ref, gates_ref, wglu_ref,
                      bglu_ref, wa_ref, wb_ref, wc_ref, wout_ref, gffn_ref, wgate_ref, wup_ref,
                      wdown_ref, gfin_ref, out_ref, y_scr, mrg_scr, x1_scr, hf_scr, act_scr,
                      *, final_norm):
    tm, d = x_ref.shape
    width = za_ref.shape[-1]
    d_ff = wdown_ref.shape[0]

    supers = tm // (SUBLANES * S5_CHUNK)
    halves = S5_CHUNK // SUBLANES
    for lb in range(width // LANES):
        ws = []
        for gi in range(S5_LANE_GROUPS):
            y3 = yvec_ref[lb * S5_LANE_GROUPS + gi].reshape(supers, SUBLANES, -1)
            ws.append(jnp.stack([y3[:, :, hf * LANES:(hf + 1) * LANES] for hf in range(halves)],
                                axis=1))
        vs = _swap_list_laneblock(ws)
        vs = _swap_list_sublane(vs)
        y_scr[:, lb * LANES:(lb + 1) * LANES] = jnp.stack(vs, axis=1).reshape(tm, LANES)

    yp = jax.nn.gelu(y_scr[...] + dskip_ref[...] * za_ref[...])
    glu = (yp * _sigmoid(_dot(yp.astype(BF16), wglu_ref[...]) + bglu_ref[...])).astype(BF16)
    for c in range(d // width):
        lanes = slice(c * width, (c + 1) * width)
        ya = _dot(glu, wa_ref[:, lanes])
        yb = _dot(sgu_ref[...], wb_ref[:, lanes])
        yc = _dot(attn_ref[...], wc_ref[:, lanes])
        g0 = gates_ref[:, c * width:(c + 1) * width].astype(F32)
        g1 = gates_ref[:, d + c * width:d + (c + 1) * width].astype(F32)
        g2 = gates_ref[:, 2 * d + c * width:2 * d + (c + 1) * width].astype(F32)
        mrg_scr[:, lanes] = (g0 * ya + g1 * yb + g2 * yc).astype(BF16)
    x1_scr[...] = x_ref[...] + _dot(mrg_scr[...], wout_ref[...])

    hf_scr[...] = _rms(x1_scr[...], gffn_ref[...]).astype(BF16)
    hf = hf_scr[...]
    for c in range(d_ff // FFN_CHUNK):
        cols = slice(c * FFN_CHUNK, (c + 1) * FFN_CHUNK)
        gate = _dot(hf, wgate_ref[:, cols])
        up = _dot(hf, wup_ref[:, cols])
        act_scr[:, cols] = (gate * _sigmoid(gate) * up).astype(BF16)
    x2 = x1_scr[...] + _dot(act_scr[...], wdown_ref[...])
    out_ref[...] = _rms(x2, gfin_ref[...]) if final_norm else x2


def _merge_ffn(x, yvec, za, d_skip, sgu, attn, gates, w_glu, b_glu, w_a, w_b, w_c, w_out, g_ffn,
               w_gate, w_up, w_down, g_final, final_norm):
    n_tok, d = x.shape
    width = za.shape[-1]
    d_ff = w_down.shape[0]
    tm = TOKEN_TILE
    n_groups = yvec.shape[0]
    tile = lambda w: pl.BlockSpec((tm, w), lambda i: (i, 0))
    return pl.pallas_call(
        functools.partial(_merge_ffn_kernel, final_norm=final_norm),
        out_shape=jax.ShapeDtypeStruct((n_tok, d), F32),
        grid=(n_tok // tm,),
        in_specs=[
            tile(d),
            pl.BlockSpec((n_groups, tm // S5_CHUNK, yvec.shape[-1]), lambda i: (0, i, 0)),
            tile(width), _resident((1, width)),
            tile(width), tile(width), tile(gates.shape[-1]),
            _resident(w_glu.shape), _resident((1, width)),
            _resident(w_a.shape), _resident(w_b.shape), _resident(w_c.shape),
            _resident(w_out.shape), _resident((1, d)),
            _resident(w_gate.shape), _resident(w_up.shape), _resident(w_down.shape),
            _resident((1, d)),
        ],
        out_specs=tile(d),
        scratch_shapes=[
            pltpu.VMEM((tm, width), F32),
            pltpu.VMEM((tm, d), BF16),
            pltpu.VMEM((tm, d), F32),
            pltpu.VMEM((tm, d), BF16),
            pltpu.VMEM((tm, d_ff), BF16),
        ],
        compiler_params=_params(1),
        name="merge_ffn",
    )(x, yvec, za, d_skip, sgu, attn, gates, w_glu, b_glu, w_a, w_b, w_c, w_out, g_ffn,
      w_gate, w_up, w_down, g_final)


def kernel(x, mem, g_mix, w_in, b_gate, a_re, a_im, log_dt, bm_re, bm_im, cm_re, cm_im, d_skip, w_glu, b_glu, g_sgu, beta_sgu, w_s, b_s, g_mem, w_kv, w_a_out, w_b_out, w_c_out, w_out, g_ffn, w_gate, w_up, w_down, g_final):
    bsz, seq, d = x.shape
    depth = w_in.shape[0]
    assert bsz == SUBLANES and seq % TOKEN_TILE == 0 and TOKEN_TILE % (SUBLANES * S5_CHUNK) == 0
    row = lambda v: v.reshape(1, -1)
    flat = lambda t: t.reshape(bsz * seq, t.shape[-1])
    for l in range(depth):
        k, v = _mem_kv(mem, row(g_mem[l]), w_kv[l].astype(BF16))
        za, uvec, sgu, attn, gates = _inproj(
            x, row(g_mix[l]), w_in[l].astype(BF16), row(b_gate[l]), row(g_sgu[l]),
            row(beta_sgu[l]), w_s[l], b_s[l].T, k, v)
        yvec = _s5_chunk(uvec, *_s5_chunk_weights(
            a_re[l], a_im[l], log_dt[l], bm_re[l], bm_im[l], cm_re[l], cm_im[l]))
        x = _merge_ffn(
            flat(x), yvec, flat(za), row(d_skip[l]), flat(sgu), flat(attn), flat(gates),
            w_glu[l].astype(BF16), row(b_glu[l]), w_a_out[l].astype(BF16),
            w_b_out[l].astype(BF16), w_c_out[l].astype(BF16), w_out[l].astype(BF16),
            row(g_ffn[l]), w_gate[l].astype(BF16), w_up[l].astype(BF16),
            w_down[l].astype(BF16), row(g_final), l == depth - 1).reshape(bsz, seq, d)
    return x
```

```python
import functools

import jax
import jax.numpy as jnp
from jax import lax
from jax.experimental import pallas as pl
from jax.experimental.pallas import tpu as pltpu

F32 = jnp.float32
BF16 = jnp.bfloat16

EPS = 1e-6

S5_GROUP_DIM = 16
SGU_GROUPS = 4
SGU_BLOCK = 128
XA_HEADS = 4
XA_HEAD_DIM = 128

LANES = 128
SUBLANES = 8
MXU_DIM = 256
VMEM_LIMIT_BYTES = 58 * 1024 * 1024

TOKEN_TILE = 512
S5_CHUNK = MXU_DIM // S5_GROUP_DIM
S5_LANE_GROUPS = LANES // S5_GROUP_DIM
S5_GROUPS_PER_STEP = 4
FFN_CHUNK = 256


def _dot(a, b):
    return jnp.dot(a, b, preferred_element_type=F32)


def _sigmoid(x):
    return 0.5 * jnp.tanh(0.5 * x) + 0.5


def _rms(xf, g):
    ms = jnp.mean(xf * xf, axis=-1, keepdims=True)
    return xf * lax.rsqrt(ms + EPS) * g


def _resident(shape):
    zeros = (0,) * len(shape)
    return pl.BlockSpec(shape, lambda *_: zeros, pipeline_mode=pl.Buffered(1))


def _params(n_axes):
    return pltpu.CompilerParams(
        dimension_semantics=("arbitrary",) * n_axes,
        vmem_limit_bytes=VMEM_LIMIT_BYTES,
    )


def _butterfly(vs, index, period, axis):
    vs = list(vs)
    size = vs[0].shape[axis]
    for d in (4, 2, 1):
        keep = (index & d) == 0
        for v in range(8):
            if v & d == 0:
                a, b = vs[v], vs[v + d]
                vs[v] = jnp.where(keep, a, pltpu.roll(b, d * period, axis=axis))
                vs[v + d] = jnp.where(keep, pltpu.roll(a, size - d * period, axis=axis), b)
    return vs


def _swap_list_sublane(vs):
    axis = vs[0].ndim - 2
    return _butterfly(vs, lax.broadcasted_iota(jnp.int32, vs[0].shape, axis), 1, axis)


def _swap_list_laneblock(vs):
    axis = vs[0].ndim - 1
    lane = lax.broadcasted_iota(jnp.int32, vs[0].shape, axis)
    return _butterfly(vs, lane // S5_GROUP_DIM, S5_GROUP_DIM, axis)


def _mem_kv_kernel(mem_ref, g_ref, w_ref, k_ref, v_ref):
    width = k_ref.shape[-1]
    h = _rms(mem_ref[0], g_ref[...]).astype(BF16)
    k_ref[0] = _dot(h, w_ref[:, :width]).astype(BF16)
    v_ref[0] = _dot(h, w_ref[:, width:]).astype(BF16)


def _mem_kv(mem, g_mem, w_kv):
    bsz, mlen, d = mem.shape
    width = w_kv.shape[1] // 2
    out = jax.ShapeDtypeStruct((bsz, mlen, width), BF16)
    return pl.pallas_call(
        _mem_kv_kernel,
        out_shape=(out, out),
        grid=(bsz,),
        in_specs=[
            pl.BlockSpec((1, mlen, d), lambda b: (b, 0, 0)),
            _resident((1, d)),
            _resident(w_kv.shape),
        ],
        out_specs=(
            pl.BlockSpec((1, mlen, width), lambda b: (b, 0, 0)),
            pl.BlockSpec((1, mlen, width), lambda b: (b, 0, 0)),
        ),
        compiler_params=_params(1),
        name="mem_kv",
    )(mem, g_mem, w_kv)


def _inproj_kernel(x_ref, gmix_ref, win_ref, bgate_ref, gsgu_ref, betasgu_ref, ws_ref, bst_ref,
                   k_ref, v_ref, za_ref, uvec_ref, sgu_ref, attn_ref, gates_ref,
                   h_scr, u_scr, vn_scr):
    tt = x_ref.shape[1]
    width = za_ref.shape[-1]

    h_scr[...] = _rms(x_ref[0], gmix_ref[...]).astype(BF16)
    h = h_scr[...]

    def proj(idx):
        return _dot(h, win_ref[:, idx * width:(idx + 1) * width])

    za = proj(0)
    za_ref[0] = za
    supers = tt // (SUBLANES * S5_CHUNK)
    za5 = za.reshape(supers, SUBLANES, S5_CHUNK // SUBLANES, SUBLANES, width)
    for lb in range(width // LANES):
        lanes = slice(lb * LANES, (lb + 1) * LANES)
        vs = [za5[:, c, :, :, lanes] for c in range(SUBLANES)]
        vs = _swap_list_sublane(vs)
        vs = _swap_list_laneblock(vs)
        for gi in range(S5_LANE_GROUPS):
            w = vs[gi]
            rows = jnp.concatenate([w[:, hf] for hf in range(S5_CHUNK // SUBLANES)], axis=-1)
            uvec_ref[lb * S5_LANE_GROUPS + gi] = rows.reshape(supers * SUBLANES, -1).astype(BF16)

    u_scr[...] = jax.nn.gelu(proj(1))
    gv = jax.nn.gelu(proj(2))
    mu = jnp.mean(gv, axis=-1, keepdims=True)
    cen = gv - mu
    var = jnp.mean(cen * cen, axis=-1, keepdims=True)
    vn_scr[...] = (cen * lax.rsqrt(var + EPS) * gsgu_ref[...] + betasgu_ref[...]).astype(BF16)

    row = lax.broadcasted_iota(jnp.int32, (SGU_BLOCK, SGU_BLOCK), 0)
    col = lax.broadcasted_iota(jnp.int32, (SGU_BLOCK, SGU_BLOCK), 1)
    tril = (row >= col).astype(F32)
    gdim = width // SGU_GROUPS
    for g in range(SGU_GROUPS):
        w_g = (ws_ref[g] * tril).astype(BF16)
        bias = bst_ref[:, g:g + 1]
        lanes = slice(g * gdim, (g + 1) * gdim)
        for r in range(tt // SGU_BLOCK):
            rows = slice(r * SGU_BLOCK, (r + 1) * SGU_BLOCK)
            sv = _dot(w_g, vn_scr[rows, lanes]) + bias
            sgu_ref[0, rows, lanes] = (u_scr[rows, lanes] * sv).astype(BF16)

    q = proj(3).astype(BF16)
    scale = XA_HEAD_DIM ** -0.5
    for hd in range(XA_HEADS):
        lanes = slice(hd * XA_HEAD_DIM, (hd + 1) * XA_HEAD_DIM)
        s = lax.dot_general(q[:, lanes], k_ref[0, :, lanes], (((1,), (1,)), ((), ())),
                            preferred_element_type=F32) * scale
        p = jnp.exp(s - jnp.max(s, axis=-1, keepdims=True))
        denom = jnp.sum(p, axis=-1, keepdims=True)
        o = _dot(p.astype(BF16), v_ref[0, :, lanes])
        attn_ref[0, :, lanes] = (o / denom).astype(BF16)

    n_gate_chunks = gates_ref.shape[-1] // width
    for c in range(n_gate_chunks):
        lanes = slice(c * width, (c + 1) * width)
        zg = proj(4 + c) + bgate_ref[:, lanes]
        gates_ref[0, :, lanes] = _sigmoid(zg).astype(BF16)


def _inproj(x, g_mix, w_in, b_gate, g_sgu, beta_sgu, w_s, b_s_t, k, v):
    bsz, seq, d = x.shape
    width = g_sgu.shape[-1]
    gate_w = b_gate.shape[-1]
    mlen = k.shape[1]
    tt = TOKEN_TILE
    n_groups = width // S5_GROUP_DIM
    tiles = seq // tt
    chunk_rows = tt // S5_CHUNK
    tile = lambda w: pl.BlockSpec((1, tt, w), lambda b, i: (b, i, 0))
    per_batch = pl.BlockSpec((1, mlen, width), lambda b, i: (b, 0, 0))
    return pl.pallas_call(
        _inproj_kernel,
        out_shape=(
            jax.ShapeDtypeStruct((bsz, seq, width), F32),
            jax.ShapeDtypeStruct((n_groups, bsz * seq // S5_CHUNK, MXU_DIM), BF16),
            jax.ShapeDtypeStruct((bsz, seq, width), BF16),
            jax.ShapeDtypeStruct((bsz, seq, width), BF16),
            jax.ShapeDtypeStruct((bsz, seq, gate_w), BF16),
        ),
        grid=(bsz, tiles),
        in_specs=[
            tile(d),
            _resident((1, d)),
            _resident(w_in.shape),
            _resident((1, gate_w)),
            _resident((1, width)),
            _resident((1, width)),
            _resident(w_s.shape),
            _resident(b_s_t.shape),
            per_batch,
            per_batch,
        ],
        out_specs=(
            tile(width),
            pl.BlockSpec((n_groups, chunk_rows, MXU_DIM), lambda b, i: (0, b * tiles + i, 0)),
            tile(width), tile(width), tile(gate_w),
        ),
        scratch_shapes=[
            pltpu.VMEM((tt, d), BF16),
            pltpu.VMEM((tt, width), F32),
            pltpu.VMEM((tt, width), BF16),
        ],
        compiler_params=_params(2),
        name="inproj",
    )(x, g_mix, w_in, b_gate, g_sgu, beta_sgu, w_s, b_s_t, k, v)


def _dot_nt_f32(a, b):
    return lax.dot_general(a, b, (((1,), (1,)), ((), ())), precision=lax.Precision.HIGHEST,
                           preferred_element_type=F32)


def _s5_group_operators(a_re, a_im, log_dt, bt_re, bt_im, c_re, c_im):
    tc, gd = S5_CHUNK, S5_GROUP_DIM
    n_state = a_re.shape[-1]
    dt = jnp.exp(log_dt)
    x, th = a_re * dt, a_im * dt
    mag = jnp.exp(x)
    ab_re, ab_im = mag * jnp.cos(th), mag * jnp.sin(th)
    nr = ab_re - 1.0
    den = a_re * a_re + a_im * a_im
    f_re = (nr * a_re + ab_im * a_im) / den
    f_im = (ab_im * a_re - nr * a_im) / den
    bb_re = f_re * bt_re - f_im * bt_im
    bb_im = f_re * bt_im + f_im * bt_re

    def powers(n):
        m = jnp.exp(n * x)
        return m * jnp.cos(n * th), m * jnp.sin(n * th)

    step = lax.broadcasted_iota(jnp.int32, (tc, n_state), 0).astype(F32)

    def scaled(w_re, w_im, n):
        p_re, p_im = powers(n)
        p_re, p_im = p_re[:, None, :], p_im[:, None, :]
        return ((p_re * w_re[None] - p_im * w_im[None]).reshape(tc * gd, n_state),
                (p_re * w_im[None] + p_im * w_re[None]).reshape(tc * gd, n_state))

    us_re, us_im = scaled(bb_re, bb_im, (tc - 1.0) - step)
    w_us = jnp.concatenate([us_re, us_im, us_im, us_re], axis=-1).astype(BF16)
    sy_re, sy_im = scaled(c_re, c_im, step + 1.0)
    w_sy_t = jnp.concatenate([sy_re, -sy_im], axis=-1).astype(BF16)
    ca_re, ca_im = scaled(c_re, c_im, step)
    kern = _dot_nt_f32(bb_re, ca_re) - _dot_nt_f32(bb_im, ca_im)
    lane = lax.broadcasted_iota(jnp.int32, kern.shape, 1)
    blocks = [kern] + [jnp.where(lane >= k * gd, pltpu.roll(kern, k * gd, axis=1), 0.0)
                       for k in range(1, tc)]
    w_conv = jnp.concatenate(blocks, axis=0).astype(BF16)

    a16_re, a16_im = powers(jnp.full((1, n_state), float(tc), F32))
    p = jnp.concatenate([a16_re, a16_re], axis=-1)
    q = jnp.concatenate([-a16_im, a16_im], axis=-1)
    return w_conv, w_us, w_sy_t, p, q


def _s5_chunk_kernel(u_ref, are_ref, aim_ref, logdt_ref, btre_ref, btim_ref, cre_ref, cim_ref,
                     y_ref, et_scr, st_scr):
    n_grp, n_rows, _ = u_ref.shape
    n_chunks = n_rows // SUBLANES
    state_w = st_scr.shape[-1]

    ops = []
    for g in range(n_grp):
        w_conv, w_us, w_sy_t, p, q = _s5_group_operators(
            are_ref[g], aim_ref[g], logdt_ref[g], btre_ref[g], btim_ref[g], cre_ref[g], cim_ref[g])
        ops.append((w_conv, w_sy_t, jnp.broadcast_to(p, (SUBLANES, state_w)),
                    jnp.broadcast_to(q, (SUBLANES, state_w))))
        e = _dot(u_ref[g], w_us)
        e4 = e.reshape(SUBLANES, n_chunks // SUBLANES, SUBLANES, 2 * state_w)
        vs = _swap_list_sublane([e4[b] for b in range(SUBLANES)])
        et_scr[g] = jnp.stack(vs, axis=1).reshape(n_chunks, SUBLANES, 2 * state_w)

    def step(c, carry):
        new = []
        for g in range(n_grp):
            s, t = carry[2 * g], carry[2 * g + 1]
            _, _, p, q = ops[g]
            st_scr[g, c] = s
            e_c = et_scr[g, c]
            new += [p * s + q * t + e_c[:, :state_w], p * t - q * s + e_c[:, state_w:]]
        return tuple(new)

    zero = jnp.zeros((SUBLANES, state_w), F32)
    lax.fori_loop(0, n_chunks, step, (zero,) * (2 * n_grp), unroll=8)

    for g in range(n_grp):
        w_conv, w_sy_t, _, _ = ops[g]
        s4 = st_scr[g].reshape(n_chunks // SUBLANES, SUBLANES, SUBLANES, state_w)
        ws = _swap_list_sublane([s4[:, s] for s in range(SUBLANES)])
        s_rows = jnp.stack(ws, axis=0).reshape(n_rows, state_w).astype(BF16)
        y_state = lax.dot_general(s_rows, w_sy_t, (((1,), (1,)), ((), ())),
                                  preferred_element_type=F32)
        y_ref[g] = _dot(u_ref[g], w_conv) + y_state


def _s5_chunk(uvec, a_re, a_im, log_dt, bt_re, bt_im, c_re, c_im):
    n_groups, n_rows, lanes = uvec.shape
    gb = S5_GROUPS_PER_STEP
    n_state = a_re.shape[-1]
    grp = lambda shape: pl.BlockSpec((gb,) + shape, lambda q: (q, 0, 0))
    return pl.pallas_call(
        _s5_chunk_kernel,
        out_shape=jax.ShapeDtypeStruct((n_groups, n_rows, lanes), F32),
        grid=(n_groups // gb,),
        in_specs=[
            grp((n_rows, lanes)),
            grp((1, n_state)), grp((1, n_state)), grp((1, 1)),
            grp(bt_re.shape[1:]), grp(bt_im.shape[1:]),
            grp(c_re.shape[1:]), grp(c_im.shape[1:]),
        ],
        out_specs=grp((n_rows, lanes)),
        scratch_shapes=[
            pltpu.VMEM((gb, n_rows // SUBLANES, SUBLANES, 4 * n_state), F32),
            pltpu.VMEM((gb, n_rows // SUBLANES, SUBLANES, 2 * n_state), F32),
        ],
        compiler_params=_params(1),
        name="s5_chunk",
    )(uvec, a_re, a_im, log_dt, bt_re, bt_im, c_re, c_im)


def _merge_ffn_kernel(x_ref, yvec_ref, za_ref, dskip_ref, sgu_ref, attn_ref, gates_ref, wglu_ref,
                      bglu_ref, wa_ref, wb_ref, wc_ref, wout_ref, gffn_ref, wgate_ref, wup_ref,
                      wdown_ref, gfin_ref, out_ref, y_scr, mrg_scr, x1_scr, hf_scr, act_scr,
                      *, final_norm):
    tm, d = x_ref.shape
    width = za_ref.shape[-1]
    d_ff = wdown_ref.shape[0]

    supers = tm // (SUBLANES * S5_CHUNK)
    halves = S5_CHUNK // SUBLANES
    for lb in range(width // LANES):
        ws = []
        for gi in range(S5_LANE_GROUPS):
            y3 = yvec_ref[lb * S5_LANE_GROUPS + gi].reshape(supers, SUBLANES, -1)
            ws.append(jnp.stack([y3[:, :, hf * LANES:(hf + 1) * LANES] for hf in range(halves)],
                                axis=1))
        vs = _swap_list_laneblock(ws)
        vs = _swap_list_sublane(vs)
        y_scr[:, lb * LANES:(lb + 1) * LANES] = jnp.stack(vs, axis=1).reshape(tm, LANES)

    yp = jax.nn.gelu(y_scr[...] + dskip_ref[...] * za_ref[...])
    glu = (yp * _sigmoid(_dot(yp.astype(BF16), wglu_ref[...]) + bglu_ref[...])).astype(BF16)
    for c in range(d // width):
        lanes = slice(c * width, (c + 1) * width)
        ya = _dot(glu, wa_ref[:, lanes])
        yb = _dot(sgu_ref[...], wb_ref[:, lanes])
        yc = _dot(attn_ref[...], wc_ref[:, lanes])
        g0 = gates_ref[:, c * width:(c + 1) * width].astype(F32)
        g1 = gates_ref[:, d + c * width:d + (c + 1) * width].astype(F32)
        g2 = gates_ref[:, 2 * d + c * width:2 * d + (c + 1) * width].astype(F32)
        mrg_scr[:, lanes] = (g0 * ya + g1 * yb + g2 * yc).astype(BF16)
    x1_scr[...] = x_ref[...] + _dot(mrg_scr[...], wout_ref[...])

    hf_scr[...] = _rms(x1_scr[...], gffn_ref[...]).astype(BF16)
    hf = hf_scr[...]
    for c in range(d_ff // FFN_CHUNK):
        cols = slice(c * FFN_CHUNK, (c + 1) * FFN_CHUNK)
        gate = _dot(hf, wgate_ref[:, cols])
        up = _dot(hf, wup_ref[:, cols])
        act_scr[:, cols] = (gate * _sigmoid(gate) * up).astype(BF16)
    x2 = x1_scr[...] + _dot(act_scr[...], wdown_ref[...])
    out_ref[...] = _rms(x2, gfin_ref[...]) if final_norm else x2


def _merge_ffn(x, yvec, za, d_skip, sgu, attn, gates, w_glu, b_glu, w_a, w_b, w_c, w_out, g_ffn,
               w_gate, w_up, w_down, g_final, final_norm):
    n_tok, d = x.shape
    width = za.shape[-1]
    d_ff = w_down.shape[0]
    tm = TOKEN_TILE
    n_groups = yvec.shape[0]
    tile = lambda w: pl.BlockSpec((tm, w), lambda i: (i, 0))
    return pl.pallas_call(
        functools.partial(_merge_ffn_kernel, final_norm=final_norm),
        out_shape=jax.ShapeDtypeStruct((n_tok, d), F32),
        grid=(n_tok // tm,),
        in_specs=[
            tile(d),
            pl.BlockSpec((n_groups, tm // S5_CHUNK, yvec.shape[-1]), lambda i: (0, i, 0)),
            tile(width), _resident((1, width)),
            tile(width), tile(width), tile(gates.shape[-1]),
            _resident(w_glu.shape), _resident((1, width)),
            _resident(w_a.shape), _resident(w_b.shape), _resident(w_c.shape),
            _resident(w_out.shape), _resident((1, d)),
            _resident(w_gate.shape), _resident(w_up.shape), _resident(w_down.shape),
            _resident((1, d)),
        ],
        out_specs=tile(d),
        scratch_shapes=[
            pltpu.VMEM((tm, width), F32),
            pltpu.VMEM((tm, d), BF16),
            pltpu.VMEM((tm, d), F32),
            pltpu.VMEM((tm, d), BF16),
            pltpu.VMEM((tm, d_ff), BF16),
        ],
        compiler_params=_params(1),
        name="merge_ffn",
    )(x, yvec, za, d_skip, sgu, attn, gates, w_glu, b_glu, w_a, w_b, w_c, w_out, g_ffn,
      w_gate, w_up, w_down, g_final)


def kernel(x, mem, g_mix, w_in, b_gate, a_re, a_im, log_dt, bm_re, bm_im, cm_re, cm_im, d_skip, w_glu, b_glu, g_sgu, beta_sgu, w_s, b_s, g_mem, w_kv, w_a_out, w_b_out, w_c_out, w_out, g_ffn, w_gate, w_up, w_down, g_final):
    bsz, seq, d = x.shape
    depth = w_in.shape[0]
    assert bsz == SUBLANES and seq % TOKEN_TILE == 0 and TOKEN_TILE % (SUBLANES * S5_CHUNK) == 0
    row = lambda v: v.reshape(1, -1)
    flat = lambda t: t.reshape(bsz * seq, t.shape[-1])
    for l in range(depth):
        k, v = _mem_kv(mem, row(g_mem[l]), w_kv[l].astype(BF16))
        za, uvec, sgu, attn, gates = _inproj(
            x, row(g_mix[l]), w_in[l].astype(BF16), row(b_gate[l]), row(g_sgu[l]),
            row(beta_sgu[l]), w_s[l], b_s[l].T, k, v)
        swap = lambda w: jnp.swapaxes(w, 1, 2)
        yvec = _s5_chunk(uvec, a_re[l][:, None, :], a_im[l][:, None, :], log_dt[l][:, None, None],
                         swap(bm_re[l]), swap(bm_im[l]), cm_re[l], cm_im[l])
        x = _merge_ffn(
            flat(x), yvec, flat(za), row(d_skip[l]), flat(sgu), flat(attn), flat(gates),
            w_glu[l].astype(BF16), row(b_glu[l]), w_a_out[l].astype(BF16),
            w_b_out[l].astype(BF16), w_c_out[l].astype(BF16), w_out[l].astype(BF16),
            row(g_ffn[l]), w_gate[l].astype(BF16), w_up[l].astype(BF16),
            w_down[l].astype(BF16), row(g_final), l == depth - 1).reshape(bsz, seq, d)
    return x
```

```python
import functools

import jax
import jax.numpy as jnp
from jax import lax
from jax.experimental import pallas as pl
from jax.experimental.pallas import tpu as pltpu

F32 = jnp.float32
BF16 = jnp.bfloat16

EPS = 1e-6

S5_GROUP_DIM = 16
SGU_GROUPS = 4
SGU_BLOCK = 128
XA_HEADS = 4
XA_HEAD_DIM = 128

LANES = 128
SUBLANES = 8
BF16_SUBLANES = 16
MXU_DIM = 256
VMEM_LIMIT_BYTES = 58 * 1024 * 1024

TOKEN_TILE = 512
S5_CHUNK = MXU_DIM // S5_GROUP_DIM
S5_LANE_GROUPS = LANES // S5_GROUP_DIM
S5_GROUPS_PER_STEP = 4
FFN_CHUNK = 256


def _dot(a, b):
    return jnp.dot(a, b, preferred_element_type=F32)


def _sigmoid(x):
    return 0.5 * jnp.tanh(0.5 * x) + 0.5


def _rms(xf, g):
    ms = jnp.mean(xf * xf, axis=-1, keepdims=True)
    return xf * lax.rsqrt(ms + EPS) * g


def _resident(shape):
    zeros = (0,) * len(shape)
    return pl.BlockSpec(shape, lambda *_: zeros, pipeline_mode=pl.Buffered(1))


def _params(n_axes):
    return pltpu.CompilerParams(
        dimension_semantics=("arbitrary",) * n_axes,
        vmem_limit_bytes=VMEM_LIMIT_BYTES,
    )


def _butterfly(vs, index, period, axis):
    vs = list(vs)
    size = vs[0].shape[axis]
    for d in (4, 2, 1):
        keep = (index & d) == 0
        for v in range(8):
            if v & d == 0:
                a, b = vs[v], vs[v + d]
                vs[v] = jnp.where(keep, a, pltpu.roll(b, d * period, axis=axis))
                vs[v + d] = jnp.where(keep, pltpu.roll(a, size - d * period, axis=axis), b)
    return vs


def _swap_list_sublane(vs):
    axis = vs[0].ndim - 2
    return _butterfly(vs, lax.broadcasted_iota(jnp.int32, vs[0].shape, axis), 1, axis)


def _swap_list_laneblock(vs):
    axis = vs[0].ndim - 1
    lane = lax.broadcasted_iota(jnp.int32, vs[0].shape, axis)
    return _butterfly(vs, lane // S5_GROUP_DIM, S5_GROUP_DIM, axis)


def _row_blocks(w, n_steps):
    rows, cols = w.shape
    assert rows % (n_steps * BF16_SUBLANES) == 0
    return pl.BlockSpec((rows // n_steps, cols), lambda i: (i, 0))


def _cast_blocks(refs):
    n = len(refs) // 2
    for src, dst in zip(refs[:n], refs[n:]):
        dst[...] = src[...].astype(BF16)


def _mem_kv_kernel(mem_ref, g_ref, w_ref, win_ref, k_ref, v_ref, winb_ref):
    width = k_ref.shape[-1]
    h = _rms(mem_ref[0], g_ref[...]).astype(BF16)
    k_ref[0] = _dot(h, w_ref[:, :width].astype(BF16)).astype(BF16)
    v_ref[0] = _dot(h, w_ref[:, width:].astype(BF16)).astype(BF16)
    _cast_blocks([win_ref, winb_ref])


def _mem_kv(mem, g_mem, w_kv, w_in):
    bsz, mlen, d = mem.shape
    width = w_kv.shape[1] // 2
    out = jax.ShapeDtypeStruct((bsz, mlen, width), BF16)
    return pl.pallas_call(
        _mem_kv_kernel,
        out_shape=(out, out, jax.ShapeDtypeStruct(w_in.shape, BF16)),
        grid=(bsz,),
        in_specs=[
            pl.BlockSpec((1, mlen, d), lambda b: (b, 0, 0)),
            _resident((1, d)),
            _resident(w_kv.shape),
            _row_blocks(w_in, bsz),
        ],
        out_specs=(
            pl.BlockSpec((1, mlen, width), lambda b: (b, 0, 0)),
            pl.BlockSpec((1, mlen, width), lambda b: (b, 0, 0)),
            _row_blocks(w_in, bsz),
        ),
        compiler_params=_params(1),
        name="mem_kv",
    )(mem, g_mem, w_kv, w_in)


def _inproj_kernel(x_ref, gmix_ref, win_ref, bgate_ref, gsgu_ref, betasgu_ref, ws_ref, bst_ref,
                   k_ref, v_ref, za_ref, uvec_ref, sgu_ref, attn_ref, gates_ref,
                   h_scr, u_scr, vn_scr):
    tt = x_ref.shape[1]
    width = za_ref.shape[-1]

    h_scr[...] = _rms(x_ref[0], gmix_ref[...]).astype(BF16)
    h = h_scr[...]

    def proj(idx):
        return _dot(h, win_ref[:, idx * width:(idx + 1) * width])

    za = proj(0)
    za_ref[0] = za
    supers = tt // (SUBLANES * S5_CHUNK)
    za5 = za.reshape(supers, SUBLANES, S5_CHUNK // SUBLANES, SUBLANES, width)
    for lb in range(width // LANES):
        lanes = slice(lb * LANES, (lb + 1) * LANES)
        vs = [za5[:, c, :, :, lanes] for c in range(SUBLANES)]
        vs = _swap_list_sublane(vs)
        vs = _swap_list_laneblock(vs)
        for gi in range(S5_LANE_GROUPS):
            w = vs[gi]
            rows = jnp.concatenate([w[:, hf] for hf in range(S5_CHUNK // SUBLANES)], axis=-1)
            uvec_ref[lb * S5_LANE_GROUPS + gi] = rows.reshape(supers * SUBLANES, -1).astype(BF16)

    u_scr[...] = jax.nn.gelu(proj(1))
    gv = jax.nn.gelu(proj(2))
    mu = jnp.mean(gv, axis=-1, keepdims=True)
    cen = gv - mu
    var = jnp.mean(cen * cen, axis=-1, keepdims=True)
    vn_scr[...] = (cen * lax.rsqrt(var + EPS) * gsgu_ref[...] + betasgu_ref[...]).astype(BF16)

    row = lax.broadcasted_iota(jnp.int32, (SGU_BLOCK, SGU_BLOCK), 0)
    col = lax.broadcasted_iota(jnp.int32, (SGU_BLOCK, SGU_BLOCK), 1)
    tril = (row >= col).astype(F32)
    gdim = width // SGU_GROUPS
    for g in range(SGU_GROUPS):
        w_g = (ws_ref[g] * tril).astype(BF16)
        bias = bst_ref[:, g:g + 1]
        lanes = slice(g * gdim, (g + 1) * gdim)
        for r in range(tt // SGU_BLOCK):
            rows = slice(r * SGU_BLOCK, (r + 1) * SGU_BLOCK)
            sv = _dot(w_g, vn_scr[rows, lanes]) + bias
            sgu_ref[0, rows, lanes] = (u_scr[rows, lanes] * sv).astype(BF16)

    q = proj(3).astype(BF16)
    scale = XA_HEAD_DIM ** -0.5
    for hd in range(XA_HEADS):
        lanes = slice(hd * XA_HEAD_DIM, (hd + 1) * XA_HEAD_DIM)
        s = lax.dot_general(q[:, lanes], k_ref[0, :, lanes], (((1,), (1,)), ((), ())),
                            preferred_element_type=F32) * scale
        p = jnp.exp(s - jnp.max(s, axis=-1, keepdims=True))
        denom = jnp.sum(p, axis=-1, keepdims=True)
        o = _dot(p.astype(BF16), v_ref[0, :, lanes])
        attn_ref[0, :, lanes] = (o / denom).astype(BF16)

    n_gate_chunks = gates_ref.shape[-1] // width
    for c in range(n_gate_chunks):
        lanes = slice(c * width, (c + 1) * width)
        zg = proj(4 + c) + bgate_ref[:, lanes]
        gates_ref[0, :, lanes] = _sigmoid(zg).astype(BF16)


def _inproj(x, g_mix, w_in, b_gate, g_sgu, beta_sgu, w_s, b_s_t, k, v):
    bsz, seq, d = x.shape
    width = g_sgu.shape[-1]
    gate_w = b_gate.shape[-1]
    mlen = k.shape[1]
    tt = TOKEN_TILE
    n_groups = width // S5_GROUP_DIM
    tiles = seq // tt
    chunk_rows = tt // S5_CHUNK
    tile = lambda w: pl.BlockSpec((1, tt, w), lambda b, i: (b, i, 0))
    per_batch = pl.BlockSpec((1, mlen, width), lambda b, i: (b, 0, 0))
    return pl.pallas_call(
        _inproj_kernel,
        out_shape=(
            jax.ShapeDtypeStruct((bsz, seq, width), F32),
            jax.ShapeDtypeStruct((n_groups, bsz * seq // S5_CHUNK, MXU_DIM), BF16),
            jax.ShapeDtypeStruct((bsz, seq, width), BF16),
            jax.ShapeDtypeStruct((bsz, seq, width), BF16),
            jax.ShapeDtypeStruct((bsz, seq, gate_w), BF16),
        ),
        grid=(bsz, tiles),
        in_specs=[
            tile(d),
            _resident((1, d)),
            _resident(w_in.shape),
            _resident((1, gate_w)),
            _resident((1, width)),
            _resident((1, width)),
            _resident(w_s.shape),
            _resident(b_s_t.shape),
            per_batch,
            per_batch,
        ],
        out_specs=(
            tile(width),
            pl.BlockSpec((n_groups, chunk_rows, MXU_DIM), lambda b, i: (0, b * tiles + i, 0)),
            tile(width), tile(width), tile(gate_w),
        ),
        scratch_shapes=[
            pltpu.VMEM((tt, d), BF16),
            pltpu.VMEM((tt, width), F32),
            pltpu.VMEM((tt, width), BF16),
        ],
        compiler_params=_params(2),
        name="inproj",
    )(x, g_mix, w_in, b_gate, g_sgu, beta_sgu, w_s, b_s_t, k, v)


def _dot_nt_f32(a, b):
    return lax.dot_general(a, b, (((1,), (1,)), ((), ())), precision=lax.Precision.HIGHEST,
                           preferred_element_type=F32)


def _s5_group_operators(a_re, a_im, log_dt, bt_re, bt_im, c_re, c_im):
    tc, gd = S5_CHUNK, S5_GROUP_DIM
    n_state = a_re.shape[-1]
    dt = jnp.exp(log_dt)
    x, th = a_re * dt, a_im * dt
    mag = jnp.exp(x)
    ab_re, ab_im = mag * jnp.cos(th), mag * jnp.sin(th)
    nr = ab_re - 1.0
    den = a_re * a_re + a_im * a_im
    f_re = (nr * a_re + ab_im * a_im) / den
    f_im = (ab_im * a_re - nr * a_im) / den
    bb_re = f_re * bt_re - f_im * bt_im
    bb_im = f_re * bt_im + f_im * bt_re

    def powers(n):
        m = jnp.exp(n * x)
        return m * jnp.cos(n * th), m * jnp.sin(n * th)

    step = lax.broadcasted_iota(jnp.int32, (tc, n_state), 0).astype(F32)

    def scaled(w_re, w_im, n):
        p_re, p_im = powers(n)
        p_re, p_im = p_re[:, None, :], p_im[:, None, :]
        return ((p_re * w_re[None] - p_im * w_im[None]).reshape(tc * gd, n_state),
                (p_re * w_im[None] + p_im * w_re[None]).reshape(tc * gd, n_state))

    us_re, us_im = scaled(bb_re, bb_im, (tc - 1.0) - step)
    w_us = jnp.concatenate([us_re, us_im, us_im, us_re], axis=-1).astype(BF16)
    sy_re, sy_im = scaled(c_re, c_im, step + 1.0)
    w_sy_t = jnp.concatenate([sy_re, -sy_im], axis=-1).astype(BF16)
    ca_re, ca_im = scaled(c_re, c_im, step)
    kern = _dot_nt_f32(bb_re, ca_re) - _dot_nt_f32(bb_im, ca_im)
    lane = lax.broadcasted_iota(jnp.int32, kern.shape, 1)
    blocks = [kern] + [jnp.where(lane >= k * gd, pltpu.roll(kern, k * gd, axis=1), 0.0)
                       for k in range(1, tc)]
    w_conv = jnp.concatenate(blocks, axis=0).astype(BF16)

    a16_re, a16_im = powers(jnp.full((1, n_state), float(tc), F32))
    p = jnp.concatenate([a16_re, a16_re], axis=-1)
    q = jnp.concatenate([-a16_im, a16_im], axis=-1)
    return w_conv, w_us, w_sy_t, p, q


def _s5_chunk_kernel(u_ref, are_ref, aim_ref, logdt_ref, btre_ref, btim_ref, cre_ref, cim_ref,
                     *refs, n_cast):
    cast_in, (y_ref, *cast_out), (et_scr, st_scr) = (
        refs[:n_cast], refs[n_cast:2 * n_cast + 1], refs[2 * n_cast + 1:])
    _cast_blocks(list(cast_in) + cast_out)

    n_grp, n_rows, _ = u_ref.shape
    n_chunks = n_rows // SUBLANES
    state_w = st_scr.shape[-1]

    ops = []
    for g in range(n_grp):
        w_conv, w_us, w_sy_t, p, q = _s5_group_operators(
            are_ref[g], aim_ref[g], logdt_ref[g], btre_ref[g], btim_ref[g], cre_ref[g], cim_ref[g])
        ops.append((w_conv, w_sy_t, jnp.broadcast_to(p, (SUBLANES, state_w)),
                    jnp.broadcast_to(q, (SUBLANES, state_w))))
        e = _dot(u_ref[g], w_us)
        e4 = e.reshape(SUBLANES, n_chunks // SUBLANES, SUBLANES, 2 * state_w)
        vs = _swap_list_sublane([e4[b] for b in range(SUBLANES)])
        et_scr[g] = jnp.stack(vs, axis=1).reshape(n_chunks, SUBLANES, 2 * state_w)

    def step(c, carry):
        new = []
        for g in range(n_grp):
            s, t = carry[2 * g], carry[2 * g + 1]
            _, _, p, q = ops[g]
            st_scr[g, c] = s
            e_c = et_scr[g, c]
            new += [p * s + q * t + e_c[:, :state_w], p * t - q * s + e_c[:, state_w:]]
        return tuple(new)

    zero = jnp.zeros((SUBLANES, state_w), F32)
    lax.fori_loop(0, n_chunks, step, (zero,) * (2 * n_grp), unroll=8)

    for g in range(n_grp):
        w_conv, w_sy_t, _, _ = ops[g]
        s4 = st_scr[g].reshape(n_chunks // SUBLANES, SUBLANES, SUBLANES, state_w)
        ws = _swap_list_sublane([s4[:, s] for s in range(SUBLANES)])
        s_rows = jnp.stack(ws, axis=0).reshape(n_rows, state_w).astype(BF16)
        y_state = lax.dot_general(s_rows, w_sy_t, (((1,), (1,)), ((), ())),
                                  preferred_element_type=F32)
        y_ref[g] = _dot(u_ref[g], w_conv) + y_state


def _s5_chunk(uvec, a_re, a_im, log_dt, bt_re, bt_im, c_re, c_im, cast_weights):
    n_groups, n_rows, lanes = uvec.shape
    gb = S5_GROUPS_PER_STEP
    n_steps = n_groups // gb
    n_state = a_re.shape[-1]
    grp = lambda shape: pl.BlockSpec((gb,) + shape, lambda q: (q, 0, 0))
    cast_specs = [_row_blocks(w, n_steps) for w in cast_weights]
    return pl.pallas_call(
        functools.partial(_s5_chunk_kernel, n_cast=len(cast_weights)),
        out_shape=[jax.ShapeDtypeStruct((n_groups, n_rows, lanes), F32)]
        + [jax.ShapeDtypeStruct(w.shape, BF16) for w in cast_weights],
        grid=(n_steps,),
        in_specs=[
            grp((n_rows, lanes)),
            grp((1, n_state)), grp((1, n_state)), grp((1, 1)),
            grp(bt_re.shape[1:]), grp(bt_im.shape[1:]),
            grp(c_re.shape[1:]), grp(c_im.shape[1:]),
        ] + cast_specs,
        out_specs=[grp((n_rows, lanes))] + cast_specs,
        scratch_shapes=[
            pltpu.VMEM((gb, n_rows // SUBLANES, SUBLANES, 4 * n_state), F32),
            pltpu.VMEM((gb, n_rows // SUBLANES, SUBLANES, 2 * n_state), F32),
        ],
        compiler_params=_params(1),
        name="s5_chunk",
    )(uvec, a_re, a_im, log_dt, bt_re, bt_im, c_re, c_im, *cast_weights)


def _merge_ffn_kernel(x_ref, yvec_ref, za_ref, dskip_ref, sgu_ref, attn_ref, gates_ref, wglu_ref,
                      bglu_ref, wa_ref, wb_ref, wc_ref, wout_ref, gffn_ref, wgate_ref, wup_ref,
                      wdown_ref, gfin_ref, out_ref, y_scr, mrg_scr, x1_scr, hf_scr, act_scr,
                      *, final_norm):
    tm, d = x_ref.shape
    width = za_ref.shape[-1]
    d_ff = wdown_ref.shape[0]

    for c in range(d // width):
        lanes = slice(c * width, (c + 1) * width)
        g1 = gates_ref[:, d + c * width:d + (c + 1) * width].astype(F32)
        g2 = gates_ref[:, 2 * d + c * width:2 * d + (c + 1) * width].astype(F32)
        x1_scr[:, lanes] = (g1 * _dot(sgu_ref[...], wb_ref[:, lanes])
                            + g2 * _dot(attn_ref[...], wc_ref[:, lanes]))

    supers = tm // (SUBLANES * S5_CHUNK)
    halves = S5_CHUNK // SUBLANES
    for lb in range(width // LANES):
        ws = []
        for gi in range(S5_LANE_GROUPS):
            y3 = yvec_ref[lb * S5_LANE_GROUPS + gi].reshape(supers, SUBLANES, -1)
            ws.append(jnp.stack([y3[:, :, hf * LANES:(hf + 1) * LANES] for hf in range(halves)],
                                axis=1))
        vs = _swap_list_laneblock(ws)
        vs = _swap_list_sublane(vs)
        y_scr[:, lb * LANES:(lb + 1) * LANES] = jnp.stack(vs, axis=1).reshape(tm, LANES)

    yp = jax.nn.gelu(y_scr[...] + dskip_ref[...] * za_ref[...])
    glu = (yp * _sigmoid(_dot(yp.astype(BF16), wglu_ref[...]) + bglu_ref[...])).astype(BF16)
    for c in range(d // width):
        lanes = slice(c * width, (c + 1) * width)
        g0 = gates_ref[:, c * width:(c + 1) * width].astype(F32)
        mrg_scr[:, lanes] = (g0 * _dot(glu, wa_ref[:, lanes]) + x1_scr[:, lanes]).astype(BF16)
    x1_scr[...] = x_ref[...] + _dot(mrg_scr[...], wout_ref[...])

    hf_scr[...] = _rms(x1_scr[...], gffn_ref[...]).astype(BF16)
    hf = hf_scr[...]
    for c in range(d_ff // FFN_CHUNK):
        cols = slice(c * FFN_CHUNK, (c + 1) * FFN_CHUNK)
        gate = _dot(hf, wgate_ref[:, cols])
        up = _dot(hf, wup_ref[:, cols])
        act_scr[:, cols] = (gate * _sigmoid(gate) * up).astype(BF16)
    x2 = x1_scr[...] + _dot(act_scr[...], wdown_ref[...])
    out_ref[...] = _rms(x2, gfin_ref[...]) if final_norm else x2


def _merge_ffn(x, yvec, za, d_skip, sgu, attn, gates, w_glu, b_glu, w_a, w_b, w_c, w_out, g_ffn,
               w_gate, w_up, w_down, g_final, final_norm):
    n_tok, d = x.shape
    width = za.shape[-1]
    d_ff = w_down.shape[0]
    tm = TOKEN_TILE
    n_groups = yvec.shape[0]
    tile = lambda w: pl.BlockSpec((tm, w), lambda i: (i, 0))
    return pl.pallas_call(
        functools.partial(_merge_ffn_kernel, final_norm=final_norm),
        out_shape=jax.ShapeDtypeStruct((n_tok, d), F32),
        grid=(n_tok // tm,),
        in_specs=[
            tile(d),
            pl.BlockSpec((n_groups, tm // S5_CHUNK, yvec.shape[-1]), lambda i: (0, i, 0)),
            tile(width), _resident((1, width)),
            tile(width), tile(width), tile(gates.shape[-1]),
            _resident(w_glu.shape), _resident((1, width)),
            _resident(w_a.shape), _resident(w_b.shape), _resident(w_c.shape),
            _resident(w_out.shape), _resident((1, d)),
            _resident(w_gate.shape), _resident(w_up.shape), _resident(w_down.shape),
            _resident((1, d)),
        ],
        out_specs=tile(d),
        scratch_shapes=[
            pltpu.VMEM((tm, width), F32),
            pltpu.VMEM((tm, d), BF16),
            pltpu.VMEM((tm, d), F32),
            pltpu.VMEM((tm, d), BF16),
            pltpu.VMEM((tm, d_ff), BF16),
        ],
        compiler_params=_params(1),
        name="merge_ffn",
    )(x, yvec, za, d_skip, sgu, attn, gates, w_glu, b_glu, w_a, w_b, w_c, w_out, g_ffn,
      w_gate, w_up, w_down, g_final)


def kernel(x, mem, g_mix, w_in, b_gate, a_re, a_im, log_dt, bm_re, bm_im, cm_re, cm_im, d_skip, w_glu, b_glu, g_sgu, beta_sgu, w_s, b_s, g_mem, w_kv, w_a_out, w_b_out, w_c_out, w_out, g_ffn, w_gate, w_up, w_down, g_final):
    bsz, seq, d = x.shape
    depth = w_in.shape[0]
    assert bsz == SUBLANES and seq % TOKEN_TILE == 0 and TOKEN_TILE % (SUBLANES * S5_CHUNK) == 0
    row = lambda v: v.reshape(1, -1)
    flat = lambda t: t.reshape(bsz * seq, t.shape[-1])
    for l in range(depth):
        k, v, w_in_b = _mem_kv(mem, row(g_mem[l]), w_kv[l], w_in[l])
        za, uvec, sgu, attn, gates = _inproj(
            x, row(g_mix[l]), w_in_b, row(b_gate[l]), row(g_sgu[l]),
            row(beta_sgu[l]), w_s[l], b_s[l].T, k, v)
        swap = lambda w: jnp.swapaxes(w, 1, 2)
        yvec, w_glu_b, w_a_b, w_b_b, w_c_b, w_out_b, w_gate_b, w_up_b, w_down_b = _s5_chunk(
            uvec, a_re[l][:, None, :], a_im[l][:, None, :], log_dt[l][:, None, None],
            swap(bm_re[l]), swap(bm_im[l]), cm_re[l], cm_im[l],
            [w_glu[l], w_a_out[l], w_b_out[l], w_c_out[l], w_out[l], w_gate[l], w_up[l], w_down[l]])
        x = _merge_ffn(
            flat(x), yvec, flat(za), row(d_skip[l]), flat(sgu), flat(attn), flat(gates),
            w_glu_b, row(b_glu[l]), w_a_b, w_b_b, w_c_b, w_out_b,
            row(g_ffn[l]), w_gate_b, w_up_b, w_down_b, row(g_final),
            l == depth - 1).reshape(bsz, seq, d)
    return x
```

```python
import functools

import jax
import jax.numpy as jnp
from jax import lax
from jax.experimental import pallas as pl
from jax.experimental.pallas import tpu as pltpu

F32 = jnp.float32
BF16 = jnp.bfloat16

EPS = 1e-6

S5_GROUP_DIM = 16
SGU_GROUPS = 4
SGU_BLOCK = 128
XA_HEADS = 4
XA_HEAD_DIM = 128

LANES = 128
SUBLANES = 8
BF16_SUBLANES = 16
MXU_DIM = 256
VMEM_LIMIT_BYTES = 58 * 1024 * 1024

INPROJ_TILE = 1024
TOKEN_TILE = 512
S5_CHUNK = MXU_DIM // S5_GROUP_DIM
S5_LANE_GROUPS = LANES // S5_GROUP_DIM
S5_GROUPS_PER_STEP = 4
FFN_CHUNK = 256


def _dot(a, b):
    return jnp.dot(a, b, preferred_element_type=F32)


def _sigmoid(x):
    return 0.5 * jnp.tanh(0.5 * x) + 0.5


def _rms(xf, g):
    ms = jnp.mean(xf * xf, axis=-1, keepdims=True)
    return xf * lax.rsqrt(ms + EPS) * g


def _resident(shape):
    zeros = (0,) * len(shape)
    return pl.BlockSpec(shape, lambda *_: zeros, pipeline_mode=pl.Buffered(1))


def _params(n_axes):
    return pltpu.CompilerParams(
        dimension_semantics=("arbitrary",) * n_axes,
        vmem_limit_bytes=VMEM_LIMIT_BYTES,
    )


def _butterfly(vs, index, period, axis):
    vs = list(vs)
    size = vs[0].shape[axis]
    for d in (4, 2, 1):
        keep = (index & d) == 0
        for v in range(8):
            if v & d == 0:
                a, b = vs[v], vs[v + d]
                vs[v] = jnp.where(keep, a, pltpu.roll(b, d * period, axis=axis))
                vs[v + d] = jnp.where(keep, pltpu.roll(a, size - d * period, axis=axis), b)
    return vs


def _swap_list_sublane(vs):
    axis = vs[0].ndim - 2
    return _butterfly(vs, lax.broadcasted_iota(jnp.int32, vs[0].shape, axis), 1, axis)


def _swap_list_laneblock(vs):
    axis = vs[0].ndim - 1
    lane = lax.broadcasted_iota(jnp.int32, vs[0].shape, axis)
    return _butterfly(vs, lane // S5_GROUP_DIM, S5_GROUP_DIM, axis)


def _row_blocks(w, n_steps):
    rows, cols = w.shape
    assert rows % (n_steps * BF16_SUBLANES) == 0
    return pl.BlockSpec((rows // n_steps, cols), lambda i: (i, 0))


def _cast_blocks(refs):
    n = len(refs) // 2
    for src, dst in zip(refs[:n], refs[n:]):
        dst[...] = src[...].astype(BF16)


def _mem_kv_kernel(mem_ref, g_ref, w_ref, win_ref, k_ref, v_ref, winb_ref):
    width = k_ref.shape[-1]
    h = _rms(mem_ref[0], g_ref[...]).astype(BF16)
    k_ref[0] = _dot(h, w_ref[:, :width].astype(BF16)).astype(BF16)
    v_ref[0] = _dot(h, w_ref[:, width:].astype(BF16)).astype(BF16)
    _cast_blocks([win_ref, winb_ref])


def _mem_kv(mem, g_mem, w_kv, w_in):
    bsz, mlen, d = mem.shape
    width = w_kv.shape[1] // 2
    out = jax.ShapeDtypeStruct((bsz, mlen, width), BF16)
    return pl.pallas_call(
        _mem_kv_kernel,
        out_shape=(out, out, jax.ShapeDtypeStruct(w_in.shape, BF16)),
        grid=(bsz,),
        in_specs=[
            pl.BlockSpec((1, mlen, d), lambda b: (b, 0, 0)),
            _resident((1, d)),
            _resident(w_kv.shape),
            _row_blocks(w_in, bsz),
        ],
        out_specs=(
            pl.BlockSpec((1, mlen, width), lambda b: (b, 0, 0)),
            pl.BlockSpec((1, mlen, width), lambda b: (b, 0, 0)),
            _row_blocks(w_in, bsz),
        ),
        compiler_params=_params(1),
        name="mem_kv",
    )(mem, g_mem, w_kv, w_in)


def _inproj_kernel(x_ref, gmix_ref, win_ref, bgate_ref, gsgu_ref, betasgu_ref, ws_ref, bst_ref,
                   k_ref, v_ref, za_ref, uvec_ref, sgu_ref, attn_ref, gates_ref,
                   h_scr, u_scr, vn_scr):
    tt = x_ref.shape[1]
    width = za_ref.shape[-1]

    h_scr[...] = _rms(x_ref[0], gmix_ref[...]).astype(BF16)
    h = h_scr[...]

    def proj(idx):
        return _dot(h, win_ref[:, idx * width:(idx + 1) * width])

    def gate_chunk(c):
        lanes = slice(c * width, (c + 1) * width)
        zg = proj(4 + c) + bgate_ref[:, lanes]
        gates_ref[0, :, lanes] = _sigmoid(zg).astype(BF16)

    gv = jax.nn.gelu(proj(2))
    mu = jnp.mean(gv, axis=-1, keepdims=True)
    cen = gv - mu
    var = jnp.mean(cen * cen, axis=-1, keepdims=True)
    vn_scr[...] = (cen * lax.rsqrt(var + EPS) * gsgu_ref[...] + betasgu_ref[...]).astype(BF16)
    gate_chunk(0)
    u_scr[...] = jax.nn.gelu(proj(1))
    gate_chunk(1)

    row = lax.broadcasted_iota(jnp.int32, (SGU_BLOCK, SGU_BLOCK), 0)
    col = lax.broadcasted_iota(jnp.int32, (SGU_BLOCK, SGU_BLOCK), 1)
    tril = (row >= col).astype(F32)
    gdim = width // SGU_GROUPS
    for g in range(SGU_GROUPS):
        w_g = (ws_ref[g] * tril).astype(BF16)
        bias = bst_ref[:, g:g + 1]
        lanes = slice(g * gdim, (g + 1) * gdim)
        for r in range(tt // SGU_BLOCK):
            rows = slice(r * SGU_BLOCK, (r + 1) * SGU_BLOCK)
            sv = _dot(w_g, vn_scr[rows, lanes]) + bias
            sgu_ref[0, rows, lanes] = (u_scr[rows, lanes] * sv).astype(BF16)

    q = proj(3).astype(BF16)
    gate_chunk(2)
    scale = XA_HEAD_DIM ** -0.5
    for hd in range(XA_HEADS):
        lanes = slice(hd * XA_HEAD_DIM, (hd + 1) * XA_HEAD_DIM)
        s = lax.dot_general(q[:, lanes], k_ref[0, :, lanes], (((1,), (1,)), ((), ())),
                            preferred_element_type=F32) * scale
        p = jnp.exp(s - jnp.max(s, axis=-1, keepdims=True))
        denom = jnp.sum(p, axis=-1, keepdims=True)
        o = _dot(p.astype(BF16), v_ref[0, :, lanes])
        attn_ref[0, :, lanes] = (o / denom).astype(BF16)
    gate_chunk(3)

    za = proj(0)
    za_ref[0] = za
    supers = tt // (SUBLANES * S5_CHUNK)
    za5 = za.reshape(supers, SUBLANES, S5_CHUNK // SUBLANES, SUBLANES, width)
    for lb in range(width // LANES):
        lanes = slice(lb * LANES, (lb + 1) * LANES)
        vs = [za5[:, c, :, :, lanes] for c in range(SUBLANES)]
        vs = _swap_list_sublane(vs)
        vs = _swap_list_laneblock(vs)
        for gi in range(S5_LANE_GROUPS):
            w = vs[gi]
            rows = jnp.concatenate([w[:, hf] for hf in range(S5_CHUNK // SUBLANES)], axis=-1)
            uvec_ref[lb * S5_LANE_GROUPS + gi] = rows.reshape(supers * SUBLANES, -1).astype(BF16)

    for c in range(4, gates_ref.shape[-1] // width):
        gate_chunk(c)


def _inproj(x, g_mix, w_in, b_gate, g_sgu, beta_sgu, w_s, b_s_t, k, v):
    bsz, seq, d = x.shape
    width = g_sgu.shape[-1]
    gate_w = b_gate.shape[-1]
    mlen = k.shape[1]
    tt = INPROJ_TILE
    n_groups = width // S5_GROUP_DIM
    tiles = seq // tt
    chunk_rows = tt // S5_CHUNK
    tile = lambda w: pl.BlockSpec((1, tt, w), lambda b, i: (b, i, 0))
    per_batch = pl.BlockSpec((1, mlen, width), lambda b, i: (b, 0, 0))
    return pl.pallas_call(
        _inproj_kernel,
        out_shape=(
            jax.ShapeDtypeStruct((bsz, seq, width), F32),
            jax.ShapeDtypeStruct((n_groups, bsz * seq // S5_CHUNK, MXU_DIM), BF16),
            jax.ShapeDtypeStruct((bsz, seq, width), BF16),
            jax.ShapeDtypeStruct((bsz, seq, width), BF16),
            jax.ShapeDtypeStruct((bsz, seq, gate_w), BF16),
        ),
        grid=(bsz, tiles),
        in_specs=[
            tile(d),
            _resident((1, d)),
            _resident(w_in.shape),
            _resident((1, gate_w)),
            _resident((1, width)),
            _resident((1, width)),
            _resident(w_s.shape),
            _resident(b_s_t.shape),
            per_batch,
            per_batch,
        ],
        out_specs=(
            tile(width),
            pl.BlockSpec((n_groups, chunk_rows, MXU_DIM), lambda b, i: (0, b * tiles + i, 0)),
            tile(width), tile(width), tile(gate_w),
        ),
        scratch_shapes=[
            pltpu.VMEM((tt, d), BF16),
            pltpu.VMEM((tt, width), F32),
            pltpu.VMEM((tt, width), BF16),
        ],
        compiler_params=_params(2),
        name="inproj",
    )(x, g_mix, w_in, b_gate, g_sgu, beta_sgu, w_s, b_s_t, k, v)


def _dot_nt_f32(a, b):
    return lax.dot_general(a, b, (((1,), (1,)), ((), ())), precision=lax.Precision.HIGHEST,
                           preferred_element_type=F32)


def _s5_group_operators(a_re, a_im, log_dt, bt_re, bt_im, c_re, c_im):
    tc, gd = S5_CHUNK, S5_GROUP_DIM
    n_state = a_re.shape[-1]
    dt = jnp.exp(log_dt)
    x, th = a_re * dt, a_im * dt
    mag = jnp.exp(x)
    ab_re, ab_im = mag * jnp.cos(th), mag * jnp.sin(th)
    nr = ab_re - 1.0
    den = a_re * a_re + a_im * a_im
    f_re = (nr * a_re + ab_im * a_im) / den
    f_im = (ab_im * a_re - nr * a_im) / den
    bb_re = f_re * bt_re - f_im * bt_im
    bb_im = f_re * bt_im + f_im * bt_re

    def powers(n):
        m = jnp.exp(n * x)
        return m * jnp.cos(n * th), m * jnp.sin(n * th)

    step = lax.broadcasted_iota(jnp.int32, (tc, n_state), 0).astype(F32)

    def scaled(w_re, w_im, n):
        p_re, p_im = powers(n)
        p_re, p_im = p_re[:, None, :], p_im[:, None, :]
        return ((p_re * w_re[None] - p_im * w_im[None]).reshape(tc * gd, n_state),
                (p_re * w_im[None] + p_im * w_re[None]).reshape(tc * gd, n_state))

    us_re, us_im = scaled(bb_re, bb_im, (tc - 1.0) - step)
    w_us = jnp.concatenate([us_re, us_im, us_im, us_re], axis=-1).astype(BF16)
    sy_re, sy_im = scaled(c_re, c_im, step + 1.0)
    w_sy_t = jnp.concatenate([sy_re, -sy_im], axis=-1).astype(BF16)
    ca_re, ca_im = scaled(c_re, c_im, step)
    kern = _dot_nt_f32(bb_re, ca_re) - _dot_nt_f32(bb_im, ca_im)
    lane = lax.broadcasted_iota(jnp.int32, kern.shape, 1)
    blocks = [kern] + [jnp.where(lane >= k * gd, pltpu.roll(kern, k * gd, axis=1), 0.0)
                       for k in range(1, tc)]
    w_conv = jnp.concatenate(blocks, axis=0).astype(BF16)

    a16_re, a16_im = powers(jnp.full((1, n_state), float(tc), F32))
    p = jnp.concatenate([a16_re, a16_re], axis=-1)
    q = jnp.concatenate([-a16_im, a16_im], axis=-1)
    return w_conv, w_us, w_sy_t, p, q


def _s5_chunk_kernel(u_ref, are_ref, aim_ref, logdt_ref, btre_ref, btim_ref, cre_ref, cim_ref,
                     *refs, n_cast):
    cast_in, (y_ref, *cast_out), (et_scr, st_scr) = (
        refs[:n_cast], refs[n_cast:2 * n_cast + 1], refs[2 * n_cast + 1:])
    _cast_blocks(list(cast_in) + cast_out)

    n_grp, n_rows, _ = u_ref.shape
    n_chunks = n_rows // SUBLANES
    state_w = st_scr.shape[-1]

    ops = []
    for g in range(n_grp):
        w_conv, w_us, w_sy_t, p, q = _s5_group_operators(
            are_ref[g], aim_ref[g], logdt_ref[g], btre_ref[g], btim_ref[g], cre_ref[g], cim_ref[g])
        ops.append((w_conv, w_sy_t, jnp.broadcast_to(p, (SUBLANES, state_w)),
                    jnp.broadcast_to(q, (SUBLANES, state_w))))
        e = _dot(u_ref[g], w_us)
        e4 = e.reshape(SUBLANES, n_chunks // SUBLANES, SUBLANES, 2 * state_w)
        vs = _swap_list_sublane([e4[b] for b in range(SUBLANES)])
        et_scr[g] = jnp.stack(vs, axis=1).reshape(n_chunks, SUBLANES, 2 * state_w)

    def step(c, carry):
        new = []
        for g in range(n_grp):
            s, t = carry[2 * g], carry[2 * g + 1]
            _, _, p, q = ops[g]
            st_scr[g, c] = s
            e_c = et_scr[g, c]
            new += [p * s + q * t + e_c[:, :state_w], p * t - q * s + e_c[:, state_w:]]
        return tuple(new)

    zero = jnp.zeros((SUBLANES, state_w), F32)
    lax.fori_loop(0, n_chunks, step, (zero,) * (2 * n_grp), unroll=8)

    for g in range(n_grp):
        w_conv, w_sy_t, _, _ = ops[g]
        s4 = st_scr[g].reshape(n_chunks // SUBLANES, SUBLANES, SUBLANES, state_w)
        ws = _swap_list_sublane([s4[:, s] for s in range(SUBLANES)])
        s_rows = jnp.stack(ws, axis=0).reshape(n_rows, state_w).astype(BF16)
        y_state = lax.dot_general(s_rows, w_sy_t, (((1,), (1,)), ((), ())),
                                  preferred_element_type=F32)
        y_ref[g] = _dot(u_ref[g], w_conv) + y_state


def _s5_chunk(uvec, a_re, a_im, log_dt, bt_re, bt_im, c_re, c_im, cast_weights):
    n_groups, n_rows, lanes = uvec.shape
    gb = S5_GROUPS_PER_STEP
    n_steps = n_groups // gb
    n_state = a_re.shape[-1]
    grp = lambda shape: pl.BlockSpec((gb,) + shape, lambda q: (q, 0, 0))
    cast_specs = [_row_blocks(w, n_steps) for w in cast_weights]
    return pl.pallas_call(
        functools.partial(_s5_chunk_kernel, n_cast=len(cast_weights)),
        out_shape=[jax.ShapeDtypeStruct((n_groups, n_rows, lanes), F32)]
        + [jax.ShapeDtypeStruct(w.shape, BF16) for w in cast_weights],
        grid=(n_steps,),
        in_specs=[
            grp((n_rows, lanes)),
            grp((1, n_state)), grp((1, n_state)), grp((1, 1)),
            grp(bt_re.shape[1:]), grp(bt_im.shape[1:]),
            grp(c_re.shape[1:]), grp(c_im.shape[1:]),
        ] + cast_specs,
        out_specs=[grp((n_rows, lanes))] + cast_specs,
        scratch_shapes=[
            pltpu.VMEM((gb, n_rows // SUBLANES, SUBLANES, 4 * n_state), F32),
            pltpu.VMEM((gb, n_rows // SUBLANES, SUBLANES, 2 * n_state), F32),
        ],
        compiler_params=_params(1),
        name="s5_chunk",
    )(uvec, a_re, a_im, log_dt, bt_re, bt_im, c_re, c_im, *cast_weights)


def _merge_ffn_kernel(x_ref, yvec_ref, za_ref, dskip_ref, sgu_ref, attn_ref, gates_ref, wglu_ref,
                      bglu_ref, wa_ref, wb_ref, wc_ref, wout_ref, gffn_ref, wgate_ref, wup_ref,
                      wdown_ref, gfin_ref, out_ref, y_scr, mrg_scr, x1_scr, hf_scr, act_scr,
                      *, final_norm):
    tm, d = x_ref.shape
    width = za_ref.shape[-1]
    d_ff = wdown_ref.shape[0]

    for c in range(d // width):
        lanes = slice(c * width, (c + 1) * width)
        g1 = gates_ref[:, d + c * width:d + (c + 1) * width].astype(F32)
        g2 = gates_ref[:, 2 * d + c * width:2 * d + (c + 1) * width].astype(F32)
        x1_scr[:, lanes] = (g1 * _dot(sgu_ref[...], wb_ref[:, lanes])
                            + g2 * _dot(attn_ref[...], wc_ref[:, lanes]))

    supers = tm // (SUBLANES * S5_CHUNK)
    halves = S5_CHUNK // SUBLANES
    for lb in range(width // LANES):
        ws = []
        for gi in range(S5_LANE_GROUPS):
            y3 = yvec_ref[lb * S5_LANE_GROUPS + gi].reshape(supers, SUBLANES, -1)
            ws.append(jnp.stack([y3[:, :, hf * LANES:(hf + 1) * LANES] for hf in range(halves)],
                                axis=1))
        vs = _swap_list_laneblock(ws)
        vs = _swap_list_sublane(vs)
        y_scr[:, lb * LANES:(lb + 1) * LANES] = jnp.stack(vs, axis=1).reshape(tm, LANES)

    yp = jax.nn.gelu(y_scr[...] + dskip_ref[...] * za_ref[...])
    glu = (yp * _sigmoid(_dot(yp.astype(BF16), wglu_ref[...]) + bglu_ref[...])).astype(BF16)
    for c in range(d // width):
        lanes = slice(c * width, (c + 1) * width)
        g0 = gates_ref[:, c * width:(c + 1) * width].astype(F32)
        mrg_scr[:, lanes] = (g0 * _dot(glu, wa_ref[:, lanes]) + x1_scr[:, lanes]).astype(BF16)
    x1_scr[...] = x_ref[...] + _dot(mrg_scr[...], wout_ref[...])

    hf_scr[...] = _rms(x1_scr[...], gffn_ref[...]).astype(BF16)
    hf = hf_scr[...]
    for c in range(d_ff // FFN_CHUNK):
        cols = slice(c * FFN_CHUNK, (c + 1) * FFN_CHUNK)
        gate = _dot(hf, wgate_ref[:, cols])
        up = _dot(hf, wup_ref[:, cols])
        act_scr[:, cols] = (gate * _sigmoid(gate) * up).astype(BF16)
    x2 = x1_scr[...] + _dot(act_scr[...], wdown_ref[...])
    out_ref[...] = _rms(x2, gfin_ref[...]) if final_norm else x2


def _merge_ffn(x, yvec, za, d_skip, sgu, attn, gates, w_glu, b_glu, w_a, w_b, w_c, w_out, g_ffn,
               w_gate, w_up, w_down, g_final, final_norm):
    n_tok, d = x.shape
    width = za.shape[-1]
    d_ff = w_down.shape[0]
    tm = TOKEN_TILE
    n_groups = yvec.shape[0]
    tile = lambda w: pl.BlockSpec((tm, w), lambda i: (i, 0))
    return pl.pallas_call(
        functools.partial(_merge_ffn_kernel, final_norm=final_norm),
        out_shape=jax.ShapeDtypeStruct((n_tok, d), F32),
        grid=(n_tok // tm,),
        in_specs=[
            tile(d),
            pl.BlockSpec((n_groups, tm // S5_CHUNK, yvec.shape[-1]), lambda i: (0, i, 0)),
            tile(width), _resident((1, width)),
            tile(width), tile(width), tile(gates.shape[-1]),
            _resident(w_glu.shape), _resident((1, width)),
            _resident(w_a.shape), _resident(w_b.shape), _resident(w_c.shape),
            _resident(w_out.shape), _resident((1, d)),
            _resident(w_gate.shape), _resident(w_up.shape), _resident(w_down.shape),
            _resident((1, d)),
        ],
        out_specs=tile(d),
        scratch_shapes=[
            pltpu.VMEM((tm, width), F32),
            pltpu.VMEM((tm, d), BF16),
            pltpu.VMEM((tm, d), F32),
            pltpu.VMEM((tm, d), BF16),
            pltpu.VMEM((tm, d_ff), BF16),
        ],
        compiler_params=_params(1),
        name="merge_ffn",
    )(x, yvec, za, d_skip, sgu, attn, gates, w_glu, b_glu, w_a, w_b, w_c, w_out, g_ffn,
      w_gate, w_up, w_down, g_final)


def kernel(x, mem, g_mix, w_in, b_gate, a_re, a_im, log_dt, bm_re, bm_im, cm_re, cm_im, d_skip, w_glu, b_glu, g_sgu, beta_sgu, w_s, b_s, g_mem, w_kv, w_a_out, w_b_out, w_c_out, w_out, g_ffn, w_gate, w_up, w_down, g_final):
    bsz, seq, d = x.shape
    depth = w_in.shape[0]
    assert bsz == SUBLANES
    for tile_rows in (INPROJ_TILE, TOKEN_TILE):
        assert seq % tile_rows == 0 and tile_rows % (SUBLANES * S5_CHUNK) == 0
    row = lambda v: v.reshape(1, -1)
    flat = lambda t: t.reshape(bsz * seq, t.shape[-1])
    for l in range(depth):
        k, v, w_in_b = _mem_kv(mem, row(g_mem[l]), w_kv[l], w_in[l])
        za, uvec, sgu, attn, gates = _inproj(
            x, row(g_mix[l]), w_in_b, row(b_gate[l]), row(g_sgu[l]),
            row(beta_sgu[l]), w_s[l], b_s[l].T, k, v)
        swap = lambda w: jnp.swapaxes(w, 1, 2)
        yvec, w_glu_b, w_a_b, w_b_b, w_c_b, w_out_b, w_gate_b, w_up_b, w_down_b = _s5_chunk(
            uvec, a_re[l][:, None, :], a_im[l][:, None, :], log_dt[l][:, None, None],
            swap(bm_re[l]), swap(bm_im[l]), cm_re[l], cm_im[l],
            [w_glu[l], w_a_out[l], w_b_out[l], w_c_out[l], w_out[l], w_gate[l], w_up[l], w_down[l]])
        x = _merge_ffn(
            flat(x), yvec, flat(za), row(d_skip[l]), flat(sgu), flat(attn), flat(gates),
            w_glu_b, row(b_glu[l]), w_a_b, w_b_b, w_c_b, w_out_b,
            row(g_ffn[l]), w_gate_b, w_up_b, w_down_b, row(g_final),
            l == depth - 1).reshape(bsz, seq, d)
    return x
```

```python
import functools

import jax
import jax.numpy as jnp
from jax import lax
from jax.experimental import pallas as pl
from jax.experimental.pallas import tpu as pltpu

F32 = jnp.float32
BF16 = jnp.bfloat16

EPS = 1e-6

S5_GROUP_DIM = 16
SGU_GROUPS = 4
SGU_BLOCK = 128
XA_HEADS = 4
XA_HEAD_DIM = 128

LANES = 128
SUBLANES = 8
BF16_SUBLANES = 16
MXU_DIM = 256
VMEM_LIMIT_BYTES = 62 * 1024 * 1024

INPROJ_TILE = 1024
TOKEN_TILE = 512
S5_CHUNK = MXU_DIM // S5_GROUP_DIM
S5_LANE_GROUPS = LANES // S5_GROUP_DIM
S5_GROUPS_PER_STEP = 4
FFN_CHUNK = 256


def _dot(a, b):
    return jnp.dot(a, b, preferred_element_type=F32)


def _sigmoid(x):
    return 0.5 * jnp.tanh(0.5 * x) + 0.5


def _rms(xf, g):
    ms = jnp.mean(xf * xf, axis=-1, keepdims=True)
    return xf * lax.rsqrt(ms + EPS) * g


def _resident(shape):
    zeros = (0,) * len(shape)
    return pl.BlockSpec(shape, lambda *_: zeros, pipeline_mode=pl.Buffered(1))


def _params(n_axes):
    return pltpu.CompilerParams(
        dimension_semantics=("arbitrary",) * n_axes,
        vmem_limit_bytes=VMEM_LIMIT_BYTES,
    )


def _butterfly(vs, index, period, axis):
    vs = list(vs)
    size = vs[0].shape[axis]
    for d in (4, 2, 1):
        keep = (index & d) == 0
        for v in range(8):
            if v & d == 0:
                a, b = vs[v], vs[v + d]
                vs[v] = jnp.where(keep, a, pltpu.roll(b, d * period, axis=axis))
                vs[v + d] = jnp.where(keep, pltpu.roll(a, size - d * period, axis=axis), b)
    return vs


def _swap_list_sublane(vs):
    axis = vs[0].ndim - 2
    return _butterfly(vs, lax.broadcasted_iota(jnp.int32, vs[0].shape, axis), 1, axis)


def _swap_list_laneblock(vs):
    axis = vs[0].ndim - 1
    lane = lax.broadcasted_iota(jnp.int32, vs[0].shape, axis)
    return _butterfly(vs, lane // S5_GROUP_DIM, S5_GROUP_DIM, axis)


def _row_blocks(w, n_steps, step_index=lambda i: i):
    rows, cols = w.shape
    assert rows % (n_steps * BF16_SUBLANES) == 0
    return pl.BlockSpec((rows // n_steps, cols), lambda *ids: (step_index(*ids), 0))


def _cast_blocks(srcs, dsts):
    for src, dst in zip(srcs, dsts):
        dst[...] = src[...].astype(BF16)


def _mem_kv_kernel(mem_ref, g_ref, w_ref, win_ref, are_ref, aim_ref, logdt_ref, btre_ref, btim_ref,
                   cre_ref, cim_ref, k_ref, v_ref, winb_ref, wconv_ref, wus_ref, wsyt_ref,
                   p_ref, q_ref):
    width = k_ref.shape[-1]
    h = _rms(mem_ref[0], g_ref[...]).astype(BF16)
    k_ref[0] = _dot(h, w_ref[:, :width].astype(BF16)).astype(BF16)
    v_ref[0] = _dot(h, w_ref[:, width:].astype(BF16)).astype(BF16)
    _cast_blocks([win_ref], [winb_ref])
    for g in range(are_ref.shape[0]):
        wconv_ref[g], wus_ref[g], wsyt_ref[g], p_ref[g], q_ref[g] = _s5_group_operators(
            are_ref[g], aim_ref[g], logdt_ref[g], btre_ref[g], btim_ref[g], cre_ref[g], cim_ref[g])


def _mem_kv(mem, g_mem, w_kv, w_in, a_re, a_im, log_dt, bt_re, bt_im, c_re, c_im):
    bsz, mlen, d = mem.shape
    width = w_kv.shape[1] // 2
    n_groups, _, n_state = a_re.shape
    gb = n_groups // bsz
    op_rows = S5_CHUNK * S5_GROUP_DIM
    grp = lambda *shape: pl.BlockSpec((gb,) + shape, lambda b: (b, 0, 0))
    kv = jax.ShapeDtypeStruct((bsz, mlen, width), BF16)
    return pl.pallas_call(
        _mem_kv_kernel,
        out_shape=(
            kv, kv, jax.ShapeDtypeStruct(w_in.shape, BF16),
            jax.ShapeDtypeStruct((n_groups, op_rows, op_rows), BF16),
            jax.ShapeDtypeStruct((n_groups, op_rows, 2 * n_state), BF16),
            jax.ShapeDtypeStruct((n_groups, op_rows, 2 * n_state), BF16),
            jax.ShapeDtypeStruct((n_groups, 1, 2 * n_state), F32),
            jax.ShapeDtypeStruct((n_groups, 1, 2 * n_state), F32),
        ),
        grid=(bsz,),
        in_specs=[
            pl.BlockSpec((1, mlen, d), lambda b: (b, 0, 0)),
            _resident((1, d)),
            _resident(w_kv.shape),
            _row_blocks(w_in, bsz),
            grp(1, n_state), grp(1, n_state), grp(1, 1),
            grp(*bt_re.shape[1:]), grp(*bt_im.shape[1:]), grp(*c_re.shape[1:]), grp(*c_im.shape[1:]),
        ],
        out_specs=(
            pl.BlockSpec((1, mlen, width), lambda b: (b, 0, 0)),
            pl.BlockSpec((1, mlen, width), lambda b: (b, 0, 0)),
            _row_blocks(w_in, bsz),
            grp(op_rows, op_rows), grp(op_rows, 2 * n_state), grp(op_rows, 2 * n_state),
            grp(1, 2 * n_state), grp(1, 2 * n_state),
        ),
        compiler_params=_params(1),
        name="mem_kv",
    )(mem, g_mem, w_kv, w_in, a_re, a_im, log_dt, bt_re, bt_im, c_re, c_im)


def _inproj_kernel(x_ref, gmix_ref, win_ref, bgate_ref, gsgu_ref, betasgu_ref, ws_ref, bst_ref,
                   k_ref, v_ref, *refs, n_cast):
    cast_in = refs[:n_cast]
    za_ref, uvec_ref, sgu_ref, attn_ref, gates_ref = refs[n_cast:n_cast + 5]
    cast_out = refs[n_cast + 5:2 * n_cast + 5]
    h_scr, u_scr, vn_scr = refs[2 * n_cast + 5:]
    _cast_blocks(cast_in, cast_out)

    tt = x_ref.shape[1]
    width = za_ref.shape[-1]

    h_scr[...] = _rms(x_ref[0], gmix_ref[...]).astype(BF16)

    def proj(idx):
        return _dot(h_scr[...], win_ref[:, idx * width:(idx + 1) * width])

    def gate_chunk(c):
        lanes = slice(c * width, (c + 1) * width)
        zg = proj(4 + c) + bgate_ref[:, lanes]
        gates_ref[0, :, lanes] = _sigmoid(zg).astype(BF16)

    gv = jax.nn.gelu(proj(2))
    mu = jnp.mean(gv, axis=-1, keepdims=True)
    cen = gv - mu
    var = jnp.mean(cen * cen, axis=-1, keepdims=True)
    vn_scr[...] = (cen * lax.rsqrt(var + EPS) * gsgu_ref[...] + betasgu_ref[...]).astype(BF16)
    gate_chunk(0)
    u_scr[...] = jax.nn.gelu(proj(1))
    gate_chunk(1)

    row = lax.broadcasted_iota(jnp.int32, (SGU_BLOCK, SGU_BLOCK), 0)
    col = lax.broadcasted_iota(jnp.int32, (SGU_BLOCK, SGU_BLOCK), 1)
    tril = (row >= col).astype(F32)
    gdim = width // SGU_GROUPS
    blocks_per_dot = MXU_DIM // SGU_BLOCK
    for g in range(SGU_GROUPS):
        w_g = (ws_ref[g] * tril).astype(BF16)
        bias = bst_ref[:, g:g + 1]
        lanes = slice(g * gdim, (g + 1) * gdim)
        for r0 in range(0, tt // SGU_BLOCK, blocks_per_dot):
            rows = [slice((r0 + j) * SGU_BLOCK, (r0 + j + 1) * SGU_BLOCK)
                    for j in range(blocks_per_dot)]
            sv = _dot(w_g, jnp.concatenate([vn_scr[r, lanes] for r in rows], axis=-1))
            for j, r in enumerate(rows):
                sv_j = sv[:, j * gdim:(j + 1) * gdim] + bias
                sgu_ref[0, r, lanes] = (u_scr[r, lanes] * sv_j).astype(BF16)

    q = proj(3).astype(BF16)
    gate_chunk(2)
    scale = XA_HEAD_DIM ** -0.5
    for hd in range(XA_HEADS):
        lanes = slice(hd * XA_HEAD_DIM, (hd + 1) * XA_HEAD_DIM)
        s = lax.dot_general(q[:, lanes], k_ref[0, :, lanes], (((1,), (1,)), ((), ())),
                            preferred_element_type=F32) * scale
        p = jnp.exp(s - jnp.max(s, axis=-1, keepdims=True))
        denom = jnp.sum(p, axis=-1, keepdims=True)
        o = _dot(p.astype(BF16), v_ref[0, :, lanes])
        attn_ref[0, :, lanes] = (o / denom).astype(BF16)
    gate_chunk(3)

    za = proj(0)
    za_ref[0] = za
    supers = tt // (SUBLANES * S5_CHUNK)
    za5 = za.reshape(supers, SUBLANES, S5_CHUNK // SUBLANES, SUBLANES, width)
    for lb in range(width // LANES):
        lanes = slice(lb * LANES, (lb + 1) * LANES)
        vs = [za5[:, c, :, :, lanes] for c in range(SUBLANES)]
        vs = _swap_list_sublane(vs)
        vs = _swap_list_laneblock(vs)
        for gi in range(S5_LANE_GROUPS):
            w = vs[gi]
            rows = jnp.concatenate([w[:, hf] for hf in range(S5_CHUNK // SUBLANES)], axis=-1)
            uvec_ref[lb * S5_LANE_GROUPS + gi] = rows.reshape(supers * SUBLANES, -1).astype(BF16)

    for c in range(4, gates_ref.shape[-1] // width):
        gate_chunk(c)


def _inproj(x, g_mix, w_in, b_gate, g_sgu, beta_sgu, w_s, b_s_t, k, v, cast_weights):
    bsz, seq, d = x.shape
    width = g_sgu.shape[-1]
    gate_w = b_gate.shape[-1]
    mlen = k.shape[1]
    tt = INPROJ_TILE
    n_groups = width // S5_GROUP_DIM
    tiles = seq // tt
    chunk_rows = tt // S5_CHUNK
    tile = lambda w: pl.BlockSpec((1, tt, w), lambda b, i: (b, i, 0))
    per_batch = pl.BlockSpec((1, mlen, width), lambda b, i: (b, 0, 0))
    cast_specs = [_row_blocks(w, bsz * tiles, lambda b, i: b * tiles + i) for w in cast_weights]
    return pl.pallas_call(
        functools.partial(_inproj_kernel, n_cast=len(cast_weights)),
        out_shape=[
            jax.ShapeDtypeStruct((bsz, seq, width), F32),
            jax.ShapeDtypeStruct((n_groups, bsz * seq // S5_CHUNK, MXU_DIM), BF16),
            jax.ShapeDtypeStruct((bsz, seq, width), BF16),
            jax.ShapeDtypeStruct((bsz, seq, width), BF16),
            jax.ShapeDtypeStruct((bsz, seq, gate_w), BF16),
        ] + [jax.ShapeDtypeStruct(w.shape, BF16) for w in cast_weights],
        grid=(bsz, tiles),
        in_specs=[
            tile(d),
            _resident((1, d)),
            _resident(w_in.shape),
            _resident((1, gate_w)),
            _resident((1, width)),
            _resident((1, width)),
            _resident(w_s.shape),
            _resident(b_s_t.shape),
            per_batch,
            per_batch,
        ] + cast_specs,
        out_specs=[
            tile(width),
            pl.BlockSpec((n_groups, chunk_rows, MXU_DIM), lambda b, i: (0, b * tiles + i, 0)),
            tile(width), tile(width), tile(gate_w),
        ] + cast_specs,
        scratch_shapes=[
            pltpu.VMEM((tt, d), BF16),
            pltpu.VMEM((tt, width), F32),
            pltpu.VMEM((tt, width), BF16),
        ],
        compiler_params=_params(2),
        name="inproj",
    )(x, g_mix, w_in, b_gate, g_sgu, beta_sgu, w_s, b_s_t, k, v, *cast_weights)


def _dot_nt_f32(a, b):
    return lax.dot_general(a, b, (((1,), (1,)), ((), ())), precision=lax.Precision.HIGHEST,
                           preferred_element_type=F32)


def _s5_group_operators(a_re, a_im, log_dt, bt_re, bt_im, c_re, c_im):
    tc, gd = S5_CHUNK, S5_GROUP_DIM
    n_state = a_re.shape[-1]
    dt = jnp.exp(log_dt)
    x, th = a_re * dt, a_im * dt
    mag = jnp.exp(x)
    ab_re, ab_im = mag * jnp.cos(th), mag * jnp.sin(th)
    nr = ab_re - 1.0
    den = a_re * a_re + a_im * a_im
    f_re = (nr * a_re + ab_im * a_im) / den
    f_im = (ab_im * a_re - nr * a_im) / den
    bb_re = f_re * bt_re - f_im * bt_im
    bb_im = f_re * bt_im + f_im * bt_re

    def powers(n):
        m = jnp.exp(n * x)
        return m * jnp.cos(n * th), m * jnp.sin(n * th)

    step = lax.broadcasted_iota(jnp.int32, (tc, n_state), 0).astype(F32)

    def scaled(w_re, w_im, n):
        p_re, p_im = powers(n)
        p_re, p_im = p_re[:, None, :], p_im[:, None, :]
        return ((p_re * w_re[None] - p_im * w_im[None]).reshape(tc * gd, n_state),
                (p_re * w_im[None] + p_im * w_re[None]).reshape(tc * gd, n_state))

    us_re, us_im = scaled(bb_re, bb_im, (tc - 1.0) - step)
    w_us = jnp.concatenate([us_re, us_im], axis=-1).astype(BF16)
    sy_re, sy_im = scaled(c_re, c_im, step + 1.0)
    w_sy_t = jnp.concatenate([sy_re, -sy_im], axis=-1).astype(BF16)
    ca_re, ca_im = scaled(c_re, c_im, step)
    kern = _dot_nt_f32(bb_re, ca_re) - _dot_nt_f32(bb_im, ca_im)
    lane = lax.broadcasted_iota(jnp.int32, kern.shape, 1)
    blocks = [kern] + [jnp.where(lane >= k * gd, pltpu.roll(kern, k * gd, axis=1), 0.0)
                       for k in range(1, tc)]
    w_conv = jnp.concatenate(blocks, axis=0).astype(BF16)

    a16_re, a16_im = powers(jnp.full((1, n_state), float(tc), F32))
    p = jnp.concatenate([a16_re, a16_re], axis=-1)
    q = jnp.concatenate([-a16_im, a16_im], axis=-1)
    return w_conv, w_us, w_sy_t, p, q


def _s5_chunk_kernel(u_ref, wconv_ref, wus_ref, wsyt_ref, p_ref, q_ref, y_ref, et_scr, st_scr):
    n_grp, n_rows, _ = u_ref.shape
    n_chunks = n_rows // SUBLANES
    state_w = st_scr.shape[-1]
    half = state_w // 2

    for g in range(n_grp):
        e = _dot(u_ref[g], wus_ref[g])
        e4 = e.reshape(SUBLANES, n_chunks // SUBLANES, SUBLANES, state_w)
        vs = _swap_list_sublane([e4[b] for b in range(SUBLANES)])
        et = jnp.stack(vs, axis=1).reshape(n_chunks, SUBLANES, state_w)
        et_scr[g] = jnp.concatenate([et, pltpu.roll(et, half, axis=2)], axis=-1)

    mult = [(jnp.broadcast_to(p_ref[g], (SUBLANES, state_w)),
             jnp.broadcast_to(q_ref[g], (SUBLANES, state_w))) for g in range(n_grp)]

    def step(c, carry):
        new = []
        for g in range(n_grp):
            s, t = carry[2 * g], carry[2 * g + 1]
            p, q = mult[g]
            st_scr[g, c] = s
            e_c = et_scr[g, c]
            new += [p * s + q * t + e_c[:, :state_w], p * t - q * s + e_c[:, state_w:]]
        return tuple(new)

    zero = jnp.zeros((SUBLANES, state_w), F32)
    lax.fori_loop(0, n_chunks, step, (zero,) * (2 * n_grp), unroll=8)

    for g in range(n_grp):
        s4 = st_scr[g].reshape(n_chunks // SUBLANES, SUBLANES, SUBLANES, state_w)
        ws = _swap_list_sublane([s4[:, s] for s in range(SUBLANES)])
        s_rows = jnp.stack(ws, axis=0).reshape(n_rows, state_w).astype(BF16)
        y_state = lax.dot_general(s_rows, wsyt_ref[g], (((1,), (1,)), ((), ())),
                                  preferred_element_type=F32)
        y_ref[g] = _dot(u_ref[g], wconv_ref[g]) + y_state


def _s5_chunk(uvec, w_conv, w_us, w_sy_t, p, q):
    n_groups, n_rows, lanes = uvec.shape
    gb = S5_GROUPS_PER_STEP
    state_w = p.shape[-1]
    grp = lambda arr: pl.BlockSpec((gb,) + arr.shape[1:], lambda i: (i, 0, 0))
    return pl.pallas_call(
        _s5_chunk_kernel,
        out_shape=jax.ShapeDtypeStruct((n_groups, n_rows, lanes), F32),
        grid=(n_groups // gb,),
        in_specs=[grp(uvec), grp(w_conv), grp(w_us), grp(w_sy_t), grp(p), grp(q)],
        out_specs=grp(uvec),
        scratch_shapes=[
            pltpu.VMEM((gb, n_rows // SUBLANES, SUBLANES, 2 * state_w), F32),
            pltpu.VMEM((gb, n_rows // SUBLANES, SUBLANES, state_w), F32),
        ],
        compiler_params=_params(1),
        name="s5_chunk",
    )(uvec, w_conv, w_us, w_sy_t, p, q)


def _merge_ffn_kernel(x_ref, yvec_ref, za_ref, dskip_ref, sgu_ref, attn_ref, gates_ref, wglu_ref,
                      bglu_ref, wa_ref, wb_ref, wc_ref, wout_ref, gffn_ref, wgate_ref, wup_ref,
                      wdown_ref, gfin_ref, out_ref, y_scr, mrg_scr, x1_scr, hf_scr, act_scr,
                      *, final_norm):
    tm, d = x_ref.shape
    width = za_ref.shape[-1]
    d_ff = wdown_ref.shape[0]

    for c in range(d // width):
        lanes = slice(c * width, (c + 1) * width)
        g1 = gates_ref[:, d + c * width:d + (c + 1) * width].astype(F32)
        g2 = gates_ref[:, 2 * d + c * width:2 * d + (c + 1) * width].astype(F32)
        x1_scr[:, lanes] = (g1 * _dot(sgu_ref[...], wb_ref[:, lanes])
                            + g2 * _dot(attn_ref[...], wc_ref[:, lanes]))

    supers = tm // (SUBLANES * S5_CHUNK)
    halves = S5_CHUNK // SUBLANES
    for lb in range(width // LANES):
        ws = []
        for gi in range(S5_LANE_GROUPS):
            y3 = yvec_ref[lb * S5_LANE_GROUPS + gi].reshape(supers, SUBLANES, -1)
            ws.append(jnp.stack([y3[:, :, hf * LANES:(hf + 1) * LANES] for hf in range(halves)],
                                axis=1))
        vs = _swap_list_laneblock(ws)
        vs = _swap_list_sublane(vs)
        y_scr[:, lb * LANES:(lb + 1) * LANES] = jnp.stack(vs, axis=1).reshape(tm, LANES)

    yp = jax.nn.gelu(y_scr[...] + dskip_ref[...] * za_ref[...])
    glu = (yp * _sigmoid(_dot(yp.astype(BF16), wglu_ref[...]) + bglu_ref[...])).astype(BF16)
    for c in range(d // width):
        lanes = slice(c * width, (c + 1) * width)
        g0 = gates_ref[:, c * width:(c + 1) * width].astype(F32)
        mrg_scr[:, lanes] = (g0 * _dot(glu, wa_ref[:, lanes]) + x1_scr[:, lanes]).astype(BF16)
    x1_scr[...] = x_ref[...] + _dot(mrg_scr[...], wout_ref[...])

    hf_scr[...] = _rms(x1_scr[...], gffn_ref[...]).astype(BF16)
    hf = hf_scr[...]
    for c in range(d_ff // FFN_CHUNK):
        cols = slice(c * FFN_CHUNK, (c + 1) * FFN_CHUNK)
        gate = _dot(hf, wgate_ref[:, cols])
        up = _dot(hf, wup_ref[:, cols])
        act_scr[:, cols] = (gate * _sigmoid(gate) * up).astype(BF16)
    x2 = x1_scr[...] + _dot(act_scr[...], wdown_ref[...])
    out_ref[...] = _rms(x2, gfin_ref[...]) if final_norm else x2


def _merge_ffn(x, yvec, za, d_skip, sgu, attn, gates, w_glu, b_glu, w_a, w_b, w_c, w_out, g_ffn,
               w_gate, w_up, w_down, g_final, final_norm):
    n_tok, d = x.shape
    width = za.shape[-1]
    d_ff = w_down.shape[0]
    tm = TOKEN_TILE
    n_groups = yvec.shape[0]
    tile = lambda w: pl.BlockSpec((tm, w), lambda i: (i, 0))
    return pl.pallas_call(
        functools.partial(_merge_ffn_kernel, final_norm=final_norm),
        out_shape=jax.ShapeDtypeStruct((n_tok, d), F32),
        grid=(n_tok // tm,),
        in_specs=[
            tile(d),
            pl.BlockSpec((n_groups, tm // S5_CHUNK, yvec.shape[-1]), lambda i: (0, i, 0)),
            tile(width), _resident((1, width)),
            tile(width), tile(width), tile(gates.shape[-1]),
            _resident(w_glu.shape), _resident((1, width)),
            _resident(w_a.shape), _resident(w_b.shape), _resident(w_c.shape),
            _resident(w_out.shape), _resident((1, d)),
            _resident(w_gate.shape), _resident(w_up.shape), _resident(w_down.shape),
            _resident((1, d)),
        ],
        out_specs=tile(d),
        scratch_shapes=[
            pltpu.VMEM((tm, width), F32),
            pltpu.VMEM((tm, d), BF16),
            pltpu.VMEM((tm, d), F32),
            pltpu.VMEM((tm, d), BF16),
            pltpu.VMEM((tm, d_ff), BF16),
        ],
        compiler_params=_params(1),
        name="merge_ffn",
    )(x, yvec, za, d_skip, sgu, attn, gates, w_glu, b_glu, w_a, w_b, w_c, w_out, g_ffn,
      w_gate, w_up, w_down, g_final)


def kernel(x, mem, g_mix, w_in, b_gate, a_re, a_im, log_dt, bm_re, bm_im, cm_re, cm_im, d_skip, w_glu, b_glu, g_sgu, beta_sgu, w_s, b_s, g_mem, w_kv, w_a_out, w_b_out, w_c_out, w_out, g_ffn, w_gate, w_up, w_down, g_final):
    bsz, seq, d = x.shape
    depth = w_in.shape[0]
    assert bsz == SUBLANES
    for tile_rows in (INPROJ_TILE, TOKEN_TILE):
        assert seq % tile_rows == 0 and tile_rows % (SUBLANES * S5_CHUNK) == 0
    row = lambda v: v.reshape(1, -1)
    flat = lambda t: t.reshape(bsz * seq, t.shape[-1])
    for l in range(depth):
        swap = lambda w: jnp.swapaxes(w, 1, 2)
        k, v, w_in_b, *s5_ops = _mem_kv(
            mem, row(g_mem[l]), w_kv[l], w_in[l], a_re[l][:, None, :], a_im[l][:, None, :],
            log_dt[l][:, None, None], swap(bm_re[l]), swap(bm_im[l]), cm_re[l], cm_im[l])
        (za, uvec, sgu, attn, gates, w_glu_b, w_a_b, w_b_b, w_c_b, w_out_b, w_gate_b, w_up_b,
         w_down_b) = _inproj(
            x, row(g_mix[l]), w_in_b, row(b_gate[l]), row(g_sgu[l]),
            row(beta_sgu[l]), w_s[l], b_s[l].T, k, v,
            [w_glu[l], w_a_out[l], w_b_out[l], w_c_out[l], w_out[l], w_gate[l], w_up[l], w_down[l]])
        yvec = _s5_chunk(uvec, *s5_ops)
        x = _merge_ffn(
            flat(x), yvec, flat(za), row(d_skip[l]), flat(sgu), flat(attn), flat(gates),
            w_glu_b, row(b_glu[l]), w_a_b, w_b_b, w_c_b, w_out_b,
            row(g_ffn[l]), w_gate_b, w_up_b, w_down_b, row(g_final),
            l == depth - 1).reshape(bsz, seq, d)
    return x
```

```python
import functools

import jax
import jax.numpy as jnp
from jax import lax
from jax.experimental import pallas as pl
from jax.experimental.pallas import tpu as pltpu

F32 = jnp.float32
BF16 = jnp.bfloat16

EPS = 1e-6

S5_GROUP_DIM = 16
SGU_GROUPS = 4
SGU_BLOCK = 128
XA_HEADS = 4
XA_HEAD_DIM = 128

LANES = 128
SUBLANES = 8
BF16_SUBLANES = 16
MXU_DIM = 256
VMEM_LIMIT_BYTES = 62 * 1024 * 1024

INPROJ_TILE = 1024
TOKEN_TILE = 512
S5_CHUNK = MXU_DIM // S5_GROUP_DIM
S5_LANE_GROUPS = LANES // S5_GROUP_DIM
S5_GROUPS_PER_STEP = 4
FFN_CHUNK = 256


def _dot(a, b):
    return jnp.dot(a, b, preferred_element_type=F32)


def _sigmoid(x):
    return 0.5 * jnp.tanh(0.5 * x) + 0.5


def _rms(xf, g):
    ms = jnp.mean(xf * xf, axis=-1, keepdims=True)
    return xf * lax.rsqrt(ms + EPS) * g


def _resident(shape):
    zeros = (0,) * len(shape)
    return pl.BlockSpec(shape, lambda *_: zeros, pipeline_mode=pl.Buffered(1))


def _params(n_axes):
    return pltpu.CompilerParams(
        dimension_semantics=("arbitrary",) * n_axes,
        vmem_limit_bytes=VMEM_LIMIT_BYTES,
    )


def _butterfly(vs, index, period, axis):
    vs = list(vs)
    size = vs[0].shape[axis]
    for d in (4, 2, 1):
        keep = (index & d) == 0
        for v in range(8):
            if v & d == 0:
                a, b = vs[v], vs[v + d]
                vs[v] = jnp.where(keep, a, pltpu.roll(b, d * period, axis=axis))
                vs[v + d] = jnp.where(keep, pltpu.roll(a, size - d * period, axis=axis), b)
    return vs


def _swap_list_sublane(vs):
    axis = vs[0].ndim - 2
    return _butterfly(vs, lax.broadcasted_iota(jnp.int32, vs[0].shape, axis), 1, axis)


def _swap_list_laneblock(vs):
    axis = vs[0].ndim - 1
    lane = lax.broadcasted_iota(jnp.int32, vs[0].shape, axis)
    return _butterfly(vs, lane // S5_GROUP_DIM, S5_GROUP_DIM, axis)


def _row_blocks(w, n_steps, step_index=lambda i: i):
    rows, cols = w.shape
    assert rows % (n_steps * BF16_SUBLANES) == 0
    return pl.BlockSpec((rows // n_steps, cols), lambda *ids: (step_index(*ids), 0))


def _cast_blocks(srcs, dsts):
    for src, dst in zip(srcs, dsts):
        dst[...] = src[...].astype(BF16)


def _mem_kv_kernel(mem_ref, g_ref, w_ref, win_ref, are_ref, aim_ref, logdt_ref, btre_ref, btim_ref,
                   cre_ref, cim_ref, k_ref, v_ref, winb_ref, wconv_ref, wus_ref, wsyt_ref,
                   p_ref, q_ref):
    width = k_ref.shape[-1]
    h = _rms(mem_ref[0], g_ref[...]).astype(BF16)
    k_ref[0] = _dot(h, w_ref[:, :width].astype(BF16)).astype(BF16)
    v_ref[0] = _dot(h, w_ref[:, width:].astype(BF16)).astype(BF16)
    _cast_blocks([win_ref], [winb_ref])
    for g in range(are_ref.shape[0]):
        wconv_ref[g], wus_ref[g], wsyt_ref[g], p_ref[g], q_ref[g] = _s5_group_operators(
            are_ref[g], aim_ref[g], logdt_ref[g], btre_ref[g], btim_ref[g], cre_ref[g], cim_ref[g])


def _mem_kv(mem, g_mem, w_kv, w_in, a_re, a_im, log_dt, bt_re, bt_im, c_re, c_im):
    bsz, mlen, d = mem.shape
    width = w_kv.shape[1] // 2
    n_groups, _, n_state = a_re.shape
    gb = n_groups // bsz
    op_rows = S5_CHUNK * S5_GROUP_DIM
    grp = lambda *shape: pl.BlockSpec((gb,) + shape, lambda b: (b, 0, 0))
    kv = jax.ShapeDtypeStruct((bsz, mlen, width), BF16)
    return pl.pallas_call(
        _mem_kv_kernel,
        out_shape=(
            kv, kv, jax.ShapeDtypeStruct(w_in.shape, BF16),
            jax.ShapeDtypeStruct((n_groups, op_rows, op_rows), BF16),
            jax.ShapeDtypeStruct((n_groups, op_rows, 2 * n_state), BF16),
            jax.ShapeDtypeStruct((n_groups, op_rows, 2 * n_state), BF16),
            jax.ShapeDtypeStruct((n_groups, 1, 2 * n_state), F32),
            jax.ShapeDtypeStruct((n_groups, 1, 2 * n_state), F32),
        ),
        grid=(bsz,),
        in_specs=[
            pl.BlockSpec((1, mlen, d), lambda b: (b, 0, 0)),
            _resident((1, d)),
            _resident(w_kv.shape),
            _row_blocks(w_in, bsz),
            grp(1, n_state), grp(1, n_state), grp(1, 1),
            grp(*bt_re.shape[1:]), grp(*bt_im.shape[1:]), grp(*c_re.shape[1:]), grp(*c_im.shape[1:]),
        ],
        out_specs=(
            pl.BlockSpec((1, mlen, width), lambda b: (b, 0, 0)),
            pl.BlockSpec((1, mlen, width), lambda b: (b, 0, 0)),
            _row_blocks(w_in, bsz),
            grp(op_rows, op_rows), grp(op_rows, 2 * n_state), grp(op_rows, 2 * n_state),
            grp(1, 2 * n_state), grp(1, 2 * n_state),
        ),
        compiler_params=_params(1),
        name="mem_kv",
    )(mem, g_mem, w_kv, w_in, a_re, a_im, log_dt, bt_re, bt_im, c_re, c_im)


def _inproj_kernel(x_ref, gmix_ref, win_ref, bgate_ref, gsgu_ref, betasgu_ref, ws_ref, bst_ref,
                   k_ref, v_ref, *refs, n_cast):
    cast_in = refs[:n_cast]
    za_ref, uvec_ref, sgu_ref, attn_ref, gates_ref = refs[n_cast:n_cast + 5]
    cast_out = refs[n_cast + 5:2 * n_cast + 5]
    h_scr, u_scr, vn_scr = refs[2 * n_cast + 5:]
    _cast_blocks(cast_in, cast_out)

    tt = x_ref.shape[1]
    width = za_ref.shape[-1]

    h_scr[...] = _rms(x_ref[0], gmix_ref[...]).astype(BF16)

    def proj(idx):
        return _dot(h_scr[...], win_ref[:, idx * width:(idx + 1) * width])

    def gate_chunk(c):
        lanes = slice(c * width, (c + 1) * width)
        zg = proj(4 + c) + bgate_ref[:, lanes]
        gates_ref[0, :, lanes] = _sigmoid(zg).astype(BF16)

    gv = jax.nn.gelu(proj(2))
    mu = jnp.mean(gv, axis=-1, keepdims=True)
    cen = gv - mu
    var = jnp.mean(cen * cen, axis=-1, keepdims=True)
    vn_scr[...] = (cen * lax.rsqrt(var + EPS) * gsgu_ref[...] + betasgu_ref[...]).astype(BF16)
    gate_chunk(0)
    u_scr[...] = jax.nn.gelu(proj(1))
    gate_chunk(1)

    row = lax.broadcasted_iota(jnp.int32, (SGU_BLOCK, SGU_BLOCK), 0)
    col = lax.broadcasted_iota(jnp.int32, (SGU_BLOCK, SGU_BLOCK), 1)
    tril = (row >= col).astype(F32)
    gdim = width // SGU_GROUPS
    blocks_per_dot = MXU_DIM // SGU_BLOCK
    for g in range(SGU_GROUPS):
        w_g = (ws_ref[g] * tril).astype(BF16)
        bias = bst_ref[:, g:g + 1]
        lanes = slice(g * gdim, (g + 1) * gdim)
        for r0 in range(0, tt // SGU_BLOCK, blocks_per_dot):
            rows = [slice((r0 + j) * SGU_BLOCK, (r0 + j + 1) * SGU_BLOCK)
                    for j in range(blocks_per_dot)]
            sv = _dot(w_g, jnp.concatenate([vn_scr[r, lanes] for r in rows], axis=-1))
            for j, r in enumerate(rows):
                sv_j = sv[:, j * gdim:(j + 1) * gdim] + bias
                sgu_ref[0, r, lanes] = (u_scr[r, lanes] * sv_j).astype(BF16)

    q = proj(3).astype(BF16)
    gate_chunk(2)
    scale = XA_HEAD_DIM ** -0.5
    for hd in range(XA_HEADS):
        lanes = slice(hd * XA_HEAD_DIM, (hd + 1) * XA_HEAD_DIM)
        s = lax.dot_general(q[:, lanes], k_ref[0, :, lanes], (((1,), (1,)), ((), ())),
                            preferred_element_type=F32) * scale
        p = jnp.exp(s - jnp.max(s, axis=-1, keepdims=True))
        denom = jnp.sum(p, axis=-1, keepdims=True)
        o = _dot(p.astype(BF16), v_ref[0, :, lanes])
        attn_ref[0, :, lanes] = (o / denom).astype(BF16)
    gate_chunk(3)

    za = proj(0)
    za_ref[0] = za
    supers = tt // (SUBLANES * S5_CHUNK)
    za5 = za.reshape(supers, SUBLANES, S5_CHUNK // SUBLANES, SUBLANES, width)
    for lb in range(width // LANES):
        lanes = slice(lb * LANES, (lb + 1) * LANES)
        vs = [za5[:, c, :, :, lanes] for c in range(SUBLANES)]
        vs = _swap_list_sublane(vs)
        vs = _swap_list_laneblock(vs)
        for gi in range(S5_LANE_GROUPS):
            w = vs[gi]
            rows = jnp.concatenate([w[:, hf] for hf in range(S5_CHUNK // SUBLANES)], axis=-1)
            uvec_ref[lb * S5_LANE_GROUPS + gi] = rows.reshape(supers * SUBLANES, -1).astype(BF16)

    for c in range(4, gates_ref.shape[-1] // width):
        gate_chunk(c)


def _inproj(x, g_mix, w_in, b_gate, g_sgu, beta_sgu, w_s, b_s_t, k, v, cast_weights):
    bsz, seq, d = x.shape
    width = g_sgu.shape[-1]
    gate_w = b_gate.shape[-1]
    mlen = k.shape[1]
    tt = INPROJ_TILE
    n_groups = width // S5_GROUP_DIM
    tiles = seq // tt
    chunk_rows = tt // S5_CHUNK
    tile = lambda w: pl.BlockSpec((1, tt, w), lambda b, i: (b, i, 0))
    per_batch = pl.BlockSpec((1, mlen, width), lambda b, i: (b, 0, 0))
    cast_specs = [_row_blocks(w, bsz * tiles, lambda b, i: b * tiles + i) for w in cast_weights]
    return pl.pallas_call(
        functools.partial(_inproj_kernel, n_cast=len(cast_weights)),
        out_shape=[
            jax.ShapeDtypeStruct((bsz, seq, width), F32),
            jax.ShapeDtypeStruct((n_groups, bsz * seq // S5_CHUNK, MXU_DIM), BF16),
            jax.ShapeDtypeStruct((bsz, seq, width), BF16),
            jax.ShapeDtypeStruct((bsz, seq, width), BF16),
            jax.ShapeDtypeStruct((bsz, seq, gate_w), BF16),
        ] + [jax.ShapeDtypeStruct(w.shape, BF16) for w in cast_weights],
        grid=(bsz, tiles),
        in_specs=[
            tile(d),
            _resident((1, d)),
            _resident(w_in.shape),
            _resident((1, gate_w)),
            _resident((1, width)),
            _resident((1, width)),
            _resident(w_s.shape),
            _resident(b_s_t.shape),
            per_batch,
            per_batch,
        ] + cast_specs,
        out_specs=[
            tile(width),
            pl.BlockSpec((n_groups, chunk_rows, MXU_DIM), lambda b, i: (0, b * tiles + i, 0)),
            tile(width), tile(width), tile(gate_w),
        ] + cast_specs,
        scratch_shapes=[
            pltpu.VMEM((tt, d), BF16),
            pltpu.VMEM((tt, width), F32),
            pltpu.VMEM((tt, width), BF16),
        ],
        compiler_params=_params(2),
        name="inproj",
    )(x, g_mix, w_in, b_gate, g_sgu, beta_sgu, w_s, b_s_t, k, v, *cast_weights)


def _dot_nt_f32(a, b):
    return lax.dot_general(a, b, (((1,), (1,)), ((), ())), precision=lax.Precision.HIGHEST,
                           preferred_element_type=F32)


def _s5_group_operators(a_re, a_im, log_dt, bt_re, bt_im, c_re, c_im):
    tc, gd = S5_CHUNK, S5_GROUP_DIM
    n_state = a_re.shape[-1]
    dt = jnp.exp(log_dt)
    x, th = a_re * dt, a_im * dt
    mag = jnp.exp(x)
    ab_re, ab_im = mag * jnp.cos(th), mag * jnp.sin(th)
    nr = ab_re - 1.0
    den = a_re * a_re + a_im * a_im
    f_re = (nr * a_re + ab_im * a_im) / den
    f_im = (ab_im * a_re - nr * a_im) / den
    bb_re = f_re * bt_re - f_im * bt_im
    bb_im = f_re * bt_im + f_im * bt_re

    def powers(n):
        m = jnp.exp(n * x)
        return m * jnp.cos(n * th), m * jnp.sin(n * th)

    step = lax.broadcasted_iota(jnp.int32, (tc, n_state), 0).astype(F32)

    def scaled(w_re, w_im, n):
        p_re, p_im = powers(n)
        p_re, p_im = p_re[:, None, :], p_im[:, None, :]
        return ((p_re * w_re[None] - p_im * w_im[None]).reshape(tc * gd, n_state),
                (p_re * w_im[None] + p_im * w_re[None]).reshape(tc * gd, n_state))

    us_re, us_im = scaled(bb_re, bb_im, (tc - 1.0) - step)
    w_us = jnp.concatenate([us_re, us_im], axis=-1).astype(BF16)
    sy_re, sy_im = scaled(c_re, c_im, step + 1.0)
    w_sy_t = jnp.concatenate([sy_re, -sy_im], axis=-1).astype(BF16)
    ca_re, ca_im = scaled(c_re, c_im, step)
    kern = _dot_nt_f32(bb_re, ca_re) - _dot_nt_f32(bb_im, ca_im)
    lane = lax.broadcasted_iota(jnp.int32, kern.shape, 1)
    blocks = [kern] + [jnp.where(lane >= k * gd, pltpu.roll(kern, k * gd, axis=1), 0.0)
                       for k in range(1, tc)]
    w_conv = jnp.concatenate(blocks, axis=0).astype(BF16)

    a16_re, a16_im = powers(jnp.full((1, n_state), float(tc), F32))
    p = jnp.concatenate([a16_re, a16_re], axis=-1)
    q = jnp.concatenate([-a16_im, a16_im], axis=-1)
    return w_conv, w_us, w_sy_t, p, q


def _s5_chunk_kernel(u_ref, wconv_ref, wus_ref, wsyt_ref, p_ref, q_ref, y_ref, et_scr, st_scr):
    n_grp, n_rows, _ = u_ref.shape
    n_chunks = n_rows // SUBLANES
    state_w = st_scr.shape[-1]
    half = state_w // 2

    for g in range(n_grp):
        e = _dot(u_ref[g], wus_ref[g])
        e4 = e.reshape(SUBLANES, n_chunks // SUBLANES, SUBLANES, state_w)
        vs = _swap_list_sublane([e4[b] for b in range(SUBLANES)])
        et = jnp.stack(vs, axis=1).reshape(n_chunks, SUBLANES, state_w)
        et_scr[g] = jnp.concatenate([et, pltpu.roll(et, half, axis=2)], axis=-1)

    mult = [(jnp.broadcast_to(p_ref[g], (SUBLANES, state_w)),
             jnp.broadcast_to(q_ref[g], (SUBLANES, state_w))) for g in range(n_grp)]

    def step(c, carry):
        new = []
        for g in range(n_grp):
            s, t = carry[2 * g], carry[2 * g + 1]
            p, q = mult[g]
            st_scr[g, c] = s
            e_c = et_scr[g, c]
            new += [p * s + q * t + e_c[:, :state_w], p * t - q * s + e_c[:, state_w:]]
        return tuple(new)

    zero = jnp.zeros((SUBLANES, state_w), F32)
    lax.fori_loop(0, n_chunks, step, (zero,) * (2 * n_grp), unroll=8)

    for g in range(n_grp):
        s4 = st_scr[g].reshape(n_chunks // SUBLANES, SUBLANES, SUBLANES, state_w)
        ws = _swap_list_sublane([s4[:, s] for s in range(SUBLANES)])
        s_rows = jnp.stack(ws, axis=0).reshape(n_rows, state_w).astype(BF16)
        y_state = lax.dot_general(s_rows, wsyt_ref[g], (((1,), (1,)), ((), ())),
                                  preferred_element_type=F32)
        y_ref[g] = _dot(u_ref[g], wconv_ref[g]) + y_state


def _s5_chunk(uvec, w_conv, w_us, w_sy_t, p, q):
    n_groups, n_rows, lanes = uvec.shape
    gb = S5_GROUPS_PER_STEP
    state_w = p.shape[-1]
    grp = lambda arr: pl.BlockSpec((gb,) + arr.shape[1:], lambda i: (i, 0, 0))
    return pl.pallas_call(
        _s5_chunk_kernel,
        out_shape=jax.ShapeDtypeStruct((n_groups, n_rows, lanes), F32),
        grid=(n_groups // gb,),
        in_specs=[grp(uvec), grp(w_conv), grp(w_us), grp(w_sy_t), grp(p), grp(q)],
        out_specs=grp(uvec),
        scratch_shapes=[
            pltpu.VMEM((gb, n_rows // SUBLANES, SUBLANES, 2 * state_w), F32),
            pltpu.VMEM((gb, n_rows // SUBLANES, SUBLANES, state_w), F32),
        ],
        compiler_params=_params(1),
        name="s5_chunk",
    )(uvec, w_conv, w_us, w_sy_t, p, q)


def _s5_regroup_block(yvec_ref, y_scr, lb):
    tm = y_scr.shape[0]
    supers = tm // (SUBLANES * S5_CHUNK)
    halves = S5_CHUNK // SUBLANES
    ws = []
    for gi in range(S5_LANE_GROUPS):
        y3 = yvec_ref[lb * S5_LANE_GROUPS + gi].reshape(supers, SUBLANES, -1)
        ws.append(jnp.stack([y3[:, :, hf * LANES:(hf + 1) * LANES] for hf in range(halves)],
                            axis=1))
    vs = _swap_list_laneblock(ws)
    vs = _swap_list_sublane(vs)
    y_scr[:, lb * LANES:(lb + 1) * LANES] = jnp.stack(vs, axis=1).reshape(tm, LANES)


def _s5_glu(y_scr, za_ref, dskip_ref, wglu_ref, bglu_ref, glu_scr):
    yp = jax.nn.gelu(y_scr[...] + dskip_ref[...] * za_ref[...])
    glu_scr[...] = (yp * _sigmoid(_dot(yp.astype(BF16), wglu_ref[...]) + bglu_ref[...])).astype(BF16)


def _merge_ffn_kernel(x_ref, yvec0_ref, za0_ref, yvecn_ref, zan_ref, dskip_ref, sgu_ref, attn_ref,
                      gates_ref, wglu_ref, bglu_ref, wa_ref, wb_ref, wc_ref, wout_ref, gffn_ref,
                      wgate_ref, wup_ref, wdown_ref, gfin_ref, out_ref,
                      y_scr, glu_scr, mrg_scr, x1_scr, x2_scr, hf_scr, act_scr, *, final_norm):
    tm, d = x_ref.shape
    width = za0_ref.shape[-1]
    d_ff = wdown_ref.shape[0]
    n_lane_blocks = width // LANES
    step = pl.program_id(0)
    n_tiles = pl.num_programs(0) - 1

    def write_out():
        x2 = x2_scr[...]
        out_ref[...] = _rms(x2, gfin_ref[...]) if final_norm else x2

    @pl.when(step == 0)
    def _():
        x2_scr[...] = jnp.zeros_like(x2_scr)
        for lb in range(n_lane_blocks):
            _s5_regroup_block(yvec0_ref, y_scr, lb)
        _s5_glu(y_scr, za0_ref, dskip_ref, wglu_ref, bglu_ref, glu_scr)

    @pl.when(step == n_tiles)
    def _():
        write_out()

    @pl.when(step < n_tiles)
    def _():
        side_jobs = [write_out]
        side_jobs += [functools.partial(_s5_regroup_block, yvecn_ref, y_scr, lb)
                      for lb in range(n_lane_blocks)]
        side_jobs += [functools.partial(_s5_glu, y_scr, zan_ref, dskip_ref, wglu_ref, bglu_ref,
                                        glu_scr)]

        for c in range(d // width):
            lanes = slice(c * width, (c + 1) * width)
            g1 = gates_ref[:, d + c * width:d + (c + 1) * width].astype(F32)
            g2 = gates_ref[:, 2 * d + c * width:2 * d + (c + 1) * width].astype(F32)
            x1_scr[:, lanes] = (g1 * _dot(sgu_ref[...], wb_ref[:, lanes])
                                + g2 * _dot(attn_ref[...], wc_ref[:, lanes]))
        for c in range(d // width):
            lanes = slice(c * width, (c + 1) * width)
            g0 = gates_ref[:, c * width:(c + 1) * width].astype(F32)
            mrg_scr[:, lanes] = (g0 * _dot(glu_scr[...], wa_ref[:, lanes])
                                 + x1_scr[:, lanes]).astype(BF16)
        x1_scr[...] = x_ref[...] + _dot(mrg_scr[...], wout_ref[...])

        hf_scr[...] = _rms(x1_scr[...], gffn_ref[...]).astype(BF16)
        n_chunks = d_ff // FFN_CHUNK
        assert len(side_jobs) <= n_chunks
        for c in range(n_chunks):
            cols = slice(c * FFN_CHUNK, (c + 1) * FFN_CHUNK)
            gate = _dot(hf_scr[...], wgate_ref[:, cols])
            up = _dot(hf_scr[...], wup_ref[:, cols])
            act_scr[:, cols] = (gate * _sigmoid(gate) * up).astype(BF16)
            if c < len(side_jobs):
                side_jobs[c]()
        x2_scr[...] = x1_scr[...] + _dot(act_scr[...], wdown_ref[...])


def _merge_ffn(x, yvec, za, d_skip, sgu, attn, gates, w_glu, b_glu, w_a, w_b, w_c, w_out, g_ffn,
               w_gate, w_up, w_down, g_final, final_norm):
    n_tok, d = x.shape
    width = za.shape[-1]
    d_ff = w_down.shape[0]
    tm = TOKEN_TILE
    n_groups = yvec.shape[0]
    n_tiles = n_tok // tm
    last = n_tiles - 1
    tile = lambda w: pl.BlockSpec((tm, w), lambda i: (jnp.minimum(i, last), 0))
    tile_next = lambda w: pl.BlockSpec((tm, w), lambda i: (jnp.minimum(i + 1, last), 0))
    yvec_block = (n_groups, tm // S5_CHUNK, yvec.shape[-1])
    return pl.pallas_call(
        functools.partial(_merge_ffn_kernel, final_norm=final_norm),
        out_shape=jax.ShapeDtypeStruct((n_tok, d), F32),
        grid=(n_tiles + 1,),
        in_specs=[
            tile(d),
            pl.BlockSpec(yvec_block, lambda i: (0, 0, 0), pipeline_mode=pl.Buffered(1)),
            pl.BlockSpec((tm, width), lambda i: (0, 0), pipeline_mode=pl.Buffered(1)),
            pl.BlockSpec(yvec_block, lambda i: (0, jnp.minimum(i + 1, last), 0)),
            tile_next(width), _resident((1, width)),
            tile(width), tile(width), tile(gates.shape[-1]),
            _resident(w_glu.shape), _resident((1, width)),
            _resident(w_a.shape), _resident(w_b.shape), _resident(w_c.shape),
            _resident(w_out.shape), _resident((1, d)),
            _resident(w_gate.shape), _resident(w_up.shape), _resident(w_down.shape),
            _resident((1, d)),
        ],
        out_specs=pl.BlockSpec((tm, d), lambda i: (jnp.maximum(i - 1, 0), 0)),
        scratch_shapes=[
            pltpu.VMEM((tm, width), F32),
            pltpu.VMEM((tm, width), BF16),
            pltpu.VMEM((tm, d), BF16),
            pltpu.VMEM((tm, d), F32),
            pltpu.VMEM((tm, d), F32),
            pltpu.VMEM((tm, d), BF16),
            pltpu.VMEM((tm, d_ff), BF16),
        ],
        compiler_params=_params(1),
        name="merge_ffn",
    )(x, yvec, za, yvec, za, d_skip, sgu, attn, gates, w_glu, b_glu, w_a, w_b, w_c, w_out, g_ffn,
      w_gate, w_up, w_down, g_final)


def kernel(x, mem, g_mix, w_in, b_gate, a_re, a_im, log_dt, bm_re, bm_im, cm_re, cm_im, d_skip, w_glu, b_glu, g_sgu, beta_sgu, w_s, b_s, g_mem, w_kv, w_a_out, w_b_out, w_c_out, w_out, g_ffn, w_gate, w_up, w_down, g_final):
    bsz, seq, d = x.shape
    depth = w_in.shape[0]
    assert bsz == SUBLANES
    for tile_rows in (INPROJ_TILE, TOKEN_TILE):
        assert seq % tile_rows == 0 and tile_rows % (SUBLANES * S5_CHUNK) == 0
    row = lambda v: v.reshape(1, -1)
    flat = lambda t: t.reshape(bsz * seq, t.shape[-1])
    for l in range(depth):
        swap = lambda w: jnp.swapaxes(w, 1, 2)
        k, v, w_in_b, *s5_ops = _mem_kv(
            mem, row(g_mem[l]), w_kv[l], w_in[l], a_re[l][:, None, :], a_im[l][:, None, :],
            log_dt[l][:, None, None], swap(bm_re[l]), swap(bm_im[l]), cm_re[l], cm_im[l])
        (za, uvec, sgu, attn, gates, w_glu_b, w_a_b, w_b_b, w_c_b, w_out_b, w_gate_b, w_up_b,
         w_down_b) = _inproj(
            x, row(g_mix[l]), w_in_b, row(b_gate[l]), row(g_sgu[l]),
            row(beta_sgu[l]), w_s[l], b_s[l].T, k, v,
            [w_glu[l], w_a_out[l], w_b_out[l], w_c_out[l], w_out[l], w_gate[l], w_up[l], w_down[l]])
        yvec = _s5_chunk(uvec, *s5_ops)
        x = _merge_ffn(
            flat(x), yvec, flat(za), row(d_skip[l]), flat(sgu), flat(attn), flat(gates),
            w_glu_b, row(b_glu[l]), w_a_b, w_b_b, w_c_b, w_out_b,
            row(g_ffn[l]), w_gate_b, w_up_b, w_down_b, row(g_final),
            l == depth - 1).reshape(bsz, seq, d)
    return x
```

```python
import functools
import math

import jax
import jax.numpy as jnp
from jax import lax
from jax.experimental import pallas as pl
from jax.experimental.pallas import tpu as pltpu

F32 = jnp.float32
BF16 = jnp.bfloat16

EPS = 1e-6

S5_GROUP_DIM = 16
SGU_GROUPS = 4
SGU_BLOCK = 128
XA_HEADS = 4
XA_HEAD_DIM = 128

LANES = 128
SUBLANES = 8
BF16_SUBLANES = 16
MXU_DIM = 256
VMEM_LIMIT_BYTES = 62 * 1024 * 1024

INPROJ_TILE = 1024
TOKEN_TILE = 512
S5_CHUNK = MXU_DIM // S5_GROUP_DIM
S5_LANE_GROUPS = LANES // S5_GROUP_DIM
S5_GROUPS_PER_STEP = 4
FFN_CHUNK = 256


def _dot(a, b):
    return jnp.dot(a, b, preferred_element_type=F32)


def _sigmoid(x):
    return 0.5 * jnp.tanh(0.5 * x) + 0.5


_GELU_C1 = (2.0 / math.pi) ** 0.5
_GELU_C2 = _GELU_C1 * 0.044715


def _gelu(x):
    half = 0.5 * x
    return half + half * jnp.tanh(x * (_GELU_C1 + _GELU_C2 * (x * x)))


def _rms(xf, g):
    ms = jnp.mean(xf * xf, axis=-1, keepdims=True)
    return xf * lax.rsqrt(ms + EPS) * g


def _resident(shape):
    zeros = (0,) * len(shape)
    return pl.BlockSpec(shape, lambda *_: zeros, pipeline_mode=pl.Buffered(1))


def _params(n_axes):
    return pltpu.CompilerParams(
        dimension_semantics=("arbitrary",) * n_axes,
        vmem_limit_bytes=VMEM_LIMIT_BYTES,
    )


def _butterfly(vs, index, period, axis):
    vs = list(vs)
    size = vs[0].shape[axis]
    for d in (4, 2, 1):
        keep = (index & d) == 0
        for v in range(8):
            if v & d == 0:
                a, b = vs[v], vs[v + d]
                vs[v] = jnp.where(keep, a, pltpu.roll(b, d * period, axis=axis))
                vs[v + d] = jnp.where(keep, pltpu.roll(a, size - d * period, axis=axis), b)
    return vs


def _swap_list_sublane(vs):
    axis = vs[0].ndim - 2
    return _butterfly(vs, lax.broadcasted_iota(jnp.int32, vs[0].shape, axis), 1, axis)


def _swap_list_laneblock(vs):
    axis = vs[0].ndim - 1
    lane = lax.broadcasted_iota(jnp.int32, vs[0].shape, axis)
    return _butterfly(vs, lane // S5_GROUP_DIM, S5_GROUP_DIM, axis)


def _row_blocks(w, n_steps, step_index=lambda i: i):
    rows, cols = w.shape
    assert rows % (n_steps * BF16_SUBLANES) == 0
    return pl.BlockSpec((rows // n_steps, cols), lambda *ids: (step_index(*ids), 0))


def _cast_blocks(srcs, dsts):
    for src, dst in zip(srcs, dsts):
        dst[...] = src[...].astype(BF16)


def _mem_kv_kernel(mem_ref, g_ref, w_ref, win_ref, are_ref, aim_ref, logdt_ref, btre_ref, btim_ref,
                   cre_ref, cim_ref, k_ref, v_ref, winb_ref, wconv_ref, wus_ref, wsyt_ref,
                   p_ref, q_ref):
    width = k_ref.shape[-1]
    h = _rms(mem_ref[0], g_ref[...]).astype(BF16)
    k_ref[0] = _dot(h, w_ref[:, :width].astype(BF16)).astype(BF16)
    v_ref[0] = _dot(h, w_ref[:, width:].astype(BF16)).astype(BF16)
    _cast_blocks([win_ref], [winb_ref])
    for g in range(are_ref.shape[0]):
        wconv_ref[g], wus_ref[g], wsyt_ref[g], p_ref[g], q_ref[g] = _s5_group_operators(
            are_ref[g], aim_ref[g], logdt_ref[g], btre_ref[g], btim_ref[g], cre_ref[g], cim_ref[g])


def _mem_kv(mem, g_mem, w_kv, w_in, a_re, a_im, log_dt, bt_re, bt_im, c_re, c_im):
    bsz, mlen, d = mem.shape
    width = w_kv.shape[1] // 2
    n_groups, _, n_state = a_re.shape
    gb = n_groups // bsz
    op_rows = S5_CHUNK * S5_GROUP_DIM
    grp = lambda *shape: pl.BlockSpec((gb,) + shape, lambda b: (b, 0, 0))
    kv = jax.ShapeDtypeStruct((bsz, mlen, width), BF16)
    return pl.pallas_call(
        _mem_kv_kernel,
        out_shape=(
            kv, kv, jax.ShapeDtypeStruct(w_in.shape, BF16),
            jax.ShapeDtypeStruct((n_groups, op_rows, op_rows), BF16),
            jax.ShapeDtypeStruct((n_groups, op_rows, 2 * n_state), BF16),
            jax.ShapeDtypeStruct((n_groups, op_rows, 2 * n_state), BF16),
            jax.ShapeDtypeStruct((n_groups, 1, 2 * n_state), F32),
            jax.ShapeDtypeStruct((n_groups, 1, 2 * n_state), F32),
        ),
        grid=(bsz,),
        in_specs=[
            pl.BlockSpec((1, mlen, d), lambda b: (b, 0, 0)),
            _resident((1, d)),
            _resident(w_kv.shape),
            _row_blocks(w_in, bsz),
            grp(1, n_state), grp(1, n_state), grp(1, 1),
            grp(*bt_re.shape[1:]), grp(*bt_im.shape[1:]), grp(*c_re.shape[1:]), grp(*c_im.shape[1:]),
        ],
        out_specs=(
            pl.BlockSpec((1, mlen, width), lambda b: (b, 0, 0)),
            pl.BlockSpec((1, mlen, width), lambda b: (b, 0, 0)),
            _row_blocks(w_in, bsz),
            grp(op_rows, op_rows), grp(op_rows, 2 * n_state), grp(op_rows, 2 * n_state),
            grp(1, 2 * n_state), grp(1, 2 * n_state),
        ),
        compiler_params=_params(1),
        name="mem_kv",
    )(mem, g_mem, w_kv, w_in, a_re, a_im, log_dt, bt_re, bt_im, c_re, c_im)


def _inproj_kernel(x_ref, gmix_ref, win_ref, bgate_ref, gsgu_ref, betasgu_ref, ws_ref, bst_ref,
                   k_ref, v_ref, *refs, n_cast):
    cast_in = refs[:n_cast]
    za_ref, uvec_ref, sgu_ref, attn_ref, gates_ref = refs[n_cast:n_cast + 5]
    cast_out = refs[n_cast + 5:2 * n_cast + 5]
    h_scr, u_scr, vn_scr = refs[2 * n_cast + 5:]
    _cast_blocks(cast_in, cast_out)

    tt = x_ref.shape[1]
    width = za_ref.shape[-1]

    h_scr[...] = _rms(x_ref[0], gmix_ref[...]).astype(BF16)

    def proj(idx):
        return _dot(h_scr[...], win_ref[:, idx * width:(idx + 1) * width])

    def gate_chunk(c):
        lanes = slice(c * width, (c + 1) * width)
        zg = proj(4 + c) + bgate_ref[:, lanes]
        gates_ref[0, :, lanes] = _sigmoid(zg).astype(BF16)

    gv = _gelu(proj(2))
    mu = jnp.mean(gv, axis=-1, keepdims=True)
    cen = gv - mu
    var = jnp.mean(cen * cen, axis=-1, keepdims=True)
    vn_scr[...] = (cen * lax.rsqrt(var + EPS) * gsgu_ref[...] + betasgu_ref[...]).astype(BF16)
    gate_chunk(0)
    u_scr[...] = _gelu(proj(1))
    gate_chunk(1)

    row = lax.broadcasted_iota(jnp.int32, (SGU_BLOCK, SGU_BLOCK), 0)
    col = lax.broadcasted_iota(jnp.int32, (SGU_BLOCK, SGU_BLOCK), 1)
    tril = (row >= col).astype(F32)
    gdim = width // SGU_GROUPS
    blocks_per_dot = MXU_DIM // SGU_BLOCK
    for g in range(SGU_GROUPS):
        w_g = (ws_ref[g] * tril).astype(BF16)
        bias = bst_ref[:, g:g + 1]
        lanes = slice(g * gdim, (g + 1) * gdim)
        for r0 in range(0, tt // SGU_BLOCK, blocks_per_dot):
            rows = [slice((r0 + j) * SGU_BLOCK, (r0 + j + 1) * SGU_BLOCK)
                    for j in range(blocks_per_dot)]
            sv = _dot(w_g, jnp.concatenate([vn_scr[r, lanes] for r in rows], axis=-1))
            for j, r in enumerate(rows):
                sv_j = sv[:, j * gdim:(j + 1) * gdim] + bias
                sgu_ref[0, r, lanes] = (u_scr[r, lanes] * sv_j).astype(BF16)

    q = proj(3).astype(BF16)
    gate_chunk(2)
    exp2_scale = XA_HEAD_DIM ** -0.5 / math.log(2.0)
    for hd in range(XA_HEADS):
        lanes = slice(hd * XA_HEAD_DIM, (hd + 1) * XA_HEAD_DIM)
        s = lax.dot_general(q[:, lanes], k_ref[0, :, lanes], (((1,), (1,)), ((), ())),
                            preferred_element_type=F32)
        p = jnp.exp2((s - jnp.max(s, axis=-1, keepdims=True)) * exp2_scale)
        denom = jnp.sum(p, axis=-1, keepdims=True)
        o = _dot(p.astype(BF16), v_ref[0, :, lanes])
        attn_ref[0, :, lanes] = (o / denom).astype(BF16)
    gate_chunk(3)

    za = proj(0)
    za_ref[0] = za
    supers = tt // (SUBLANES * S5_CHUNK)
    za5 = za.reshape(supers, SUBLANES, S5_CHUNK // SUBLANES, SUBLANES, width)
    for lb in range(width // LANES):
        lanes = slice(lb * LANES, (lb + 1) * LANES)
        vs = [za5[:, c, :, :, lanes] for c in range(SUBLANES)]
        vs = _swap_list_sublane(vs)
        vs = _swap_list_laneblock(vs)
        for gi in range(S5_LANE_GROUPS):
            w = vs[gi]
            rows = jnp.concatenate([w[:, hf] for hf in range(S5_CHUNK // SUBLANES)], axis=-1)
            uvec_ref[lb * S5_LANE_GROUPS + gi] = rows.reshape(supers * SUBLANES, -1).astype(BF16)

    for c in range(4, gates_ref.shape[-1] // width):
        gate_chunk(c)


def _inproj(x, g_mix, w_in, b_gate, g_sgu, beta_sgu, w_s, b_s_t, k, v, cast_weights):
    bsz, seq, d = x.shape
    width = g_sgu.shape[-1]
    gate_w = b_gate.shape[-1]
    mlen = k.shape[1]
    tt = INPROJ_TILE
    n_groups = width // S5_GROUP_DIM
    tiles = seq // tt
    chunk_rows = tt // S5_CHUNK
    tile = lambda w: pl.BlockSpec((1, tt, w), lambda b, i: (b, i, 0))
    per_batch = pl.BlockSpec((1, mlen, width), lambda b, i: (b, 0, 0))
    cast_specs = [_row_blocks(w, bsz * tiles, lambda b, i: b * tiles + i) for w in cast_weights]
    return pl.pallas_call(
        functools.partial(_inproj_kernel, n_cast=len(cast_weights)),
        out_shape=[
            jax.ShapeDtypeStruct((bsz, seq, width), F32),
            jax.ShapeDtypeStruct((n_groups, bsz * seq // S5_CHUNK, MXU_DIM), BF16),
            jax.ShapeDtypeStruct((bsz, seq, width), BF16),
            jax.ShapeDtypeStruct((bsz, seq, width), BF16),
            jax.ShapeDtypeStruct((bsz, seq, gate_w), BF16),
        ] + [jax.ShapeDtypeStruct(w.shape, BF16) for w in cast_weights],
        grid=(bsz, tiles),
        in_specs=[
            tile(d),
            _resident((1, d)),
            _resident(w_in.shape),
            _resident((1, gate_w)),
            _resident((1, width)),
            _resident((1, width)),
            _resident(w_s.shape),
            _resident(b_s_t.shape),
            per_batch,
            per_batch,
        ] + cast_specs,
        out_specs=[
            tile(width),
            pl.BlockSpec((n_groups, chunk_rows, MXU_DIM), lambda b, i: (0, b * tiles + i, 0)),
            tile(width), tile(width), tile(gate_w),
        ] + cast_specs,
        scratch_shapes=[
            pltpu.VMEM((tt, d), BF16),
            pltpu.VMEM((tt, width), F32),
            pltpu.VMEM((tt, width), BF16),
        ],
        compiler_params=_params(2),
        name="inproj",
    )(x, g_mix, w_in, b_gate, g_sgu, beta_sgu, w_s, b_s_t, k, v, *cast_weights)


def _dot_nt_f32(a, b):
    def split(w):
        hi = w.astype(BF16)
        return hi, (w - hi.astype(F32)).astype(BF16)

    def nt(u, v):
        return lax.dot_general(u, v, (((1,), (1,)), ((), ())), preferred_element_type=F32)

    a_hi, a_lo = split(a)
    b_hi, b_lo = split(b)
    return nt(a_hi, b_hi) + (nt(a_hi, b_lo) + nt(a_lo, b_hi))


def _s5_group_operators(a_re, a_im, log_dt, bt_re, bt_im, c_re, c_im):
    tc, gd = S5_CHUNK, S5_GROUP_DIM
    n_state = a_re.shape[-1]
    dt = jnp.exp(log_dt)
    x, th = a_re * dt, a_im * dt
    n = lax.broadcasted_iota(jnp.int32, (tc + SUBLANES, n_state), 0).astype(F32)
    pw_mag = jnp.exp(n * x)
    pw_re, pw_im = pw_mag * jnp.cos(n * th), pw_mag * jnp.sin(n * th)
    ab_re, ab_im = pw_re[1:2], pw_im[1:2]
    nr = ab_re - 1.0
    den = a_re * a_re + a_im * a_im
    f_re = (nr * a_re + ab_im * a_im) / den
    f_im = (ab_im * a_re - nr * a_im) / den
    bb_re = f_re * bt_re - f_im * bt_im
    bb_im = f_re * bt_im + f_im * bt_re

    def scaled(w_re, w_im, powers):
        re, im = [], []
        for j in powers:
            p_re, p_im = pw_re[j:j + 1], pw_im[j:j + 1]
            re.append(p_re * w_re - p_im * w_im)
            im.append(p_re * w_im + p_im * w_re)
        return jnp.concatenate(re, axis=0), jnp.concatenate(im, axis=0)

    us_re, us_im = scaled(bb_re, bb_im, [tc - 1 - k for k in range(tc)])
    w_us = jnp.concatenate([us_re, us_im], axis=-1).astype(BF16)
    sy_re, sy_im = scaled(c_re, c_im, [i + 1 for i in range(tc)])
    w_sy_t = jnp.concatenate([sy_re, -sy_im], axis=-1).astype(BF16)
    ca_re, ca_im = scaled(c_re, c_im, range(tc))
    kern = _dot_nt_f32(bb_re, ca_re) - _dot_nt_f32(bb_im, ca_im)
    lane = lax.broadcasted_iota(jnp.int32, kern.shape, 1)
    blocks = [kern] + [jnp.where(lane >= k * gd, pltpu.roll(kern, k * gd, axis=1), 0.0)
                       for k in range(1, tc)]
    w_conv = jnp.concatenate(blocks, axis=0).astype(BF16)

    a16_re, a16_im = pw_re[tc:tc + 1], pw_im[tc:tc + 1]
    p = jnp.concatenate([a16_re, a16_re], axis=-1)
    q = jnp.concatenate([-a16_im, a16_im], axis=-1)
    return w_conv, w_us, w_sy_t, p, q


def _s5_chunk_kernel(u_ref, wconv_ref, wus_ref, wsyt_ref, p_ref, q_ref, y_ref, et_scr, st_scr):
    n_grp, n_rows, _ = u_ref.shape
    n_chunks = n_rows // SUBLANES
    state_w = st_scr.shape[-1]
    half = state_w // 2

    for g in range(n_grp):
        e = _dot(u_ref[g], wus_ref[g])
        e4 = e.reshape(SUBLANES, n_chunks // SUBLANES, SUBLANES, state_w)
        vs = _swap_list_sublane([e4[b] for b in range(SUBLANES)])
        et = jnp.stack(vs, axis=1).reshape(n_chunks, SUBLANES, state_w)
        et_scr[g] = jnp.concatenate([et, pltpu.roll(et, half, axis=2)], axis=-1)

    mult = [(jnp.broadcast_to(p_ref[g], (SUBLANES, state_w)),
             jnp.broadcast_to(q_ref[g], (SUBLANES, state_w))) for g in range(n_grp)]

    def step(c, carry):
        new = []
        for g in range(n_grp):
            s, t = carry[2 * g], carry[2 * g + 1]
            p, q = mult[g]
            st_scr[g, c] = s
            e_c = et_scr[g, c]
            new += [p * s + q * t + e_c[:, :state_w], p * t - q * s + e_c[:, state_w:]]
        return tuple(new)

    zero = jnp.zeros((SUBLANES, state_w), F32)
    lax.fori_loop(0, n_chunks, step, (zero,) * (2 * n_grp), unroll=8)

    for g in range(n_grp):
        s4 = st_scr[g].reshape(n_chunks // SUBLANES, SUBLANES, SUBLANES, state_w)
        ws = _swap_list_sublane([s4[:, s] for s in range(SUBLANES)])
        s_rows = jnp.stack(ws, axis=0).reshape(n_rows, state_w).astype(BF16)
        y_state = lax.dot_general(s_rows, wsyt_ref[g], (((1,), (1,)), ((), ())),
                                  preferred_element_type=F32)
        y_ref[g] = _dot(u_ref[g], wconv_ref[g]) + y_state


def _s5_chunk(uvec, w_conv, w_us, w_sy_t, p, q):
    n_groups, n_rows, lanes = uvec.shape
    gb = S5_GROUPS_PER_STEP
    state_w = p.shape[-1]
    grp = lambda arr: pl.BlockSpec((gb,) + arr.shape[1:], lambda i: (i, 0, 0))
    return pl.pallas_call(
        _s5_chunk_kernel,
        out_shape=jax.ShapeDtypeStruct((n_groups, n_rows, lanes), F32),
        grid=(n_groups // gb,),
        in_specs=[grp(uvec), grp(w_conv), grp(w_us), grp(w_sy_t), grp(p), grp(q)],
        out_specs=grp(uvec),
        scratch_shapes=[
            pltpu.VMEM((gb, n_rows // SUBLANES, SUBLANES, 2 * state_w), F32),
            pltpu.VMEM((gb, n_rows // SUBLANES, SUBLANES, state_w), F32),
        ],
        compiler_params=_params(1),
        name="s5_chunk",
    )(uvec, w_conv, w_us, w_sy_t, p, q)


def _s5_regroup_block(yvec_ref, y_scr, lb):
    tm = y_scr.shape[0]
    supers = tm // (SUBLANES * S5_CHUNK)
    halves = S5_CHUNK // SUBLANES
    ws = []
    for gi in range(S5_LANE_GROUPS):
        y3 = yvec_ref[lb * S5_LANE_GROUPS + gi].reshape(supers, SUBLANES, -1)
        ws.append(jnp.stack([y3[:, :, hf * LANES:(hf + 1) * LANES] for hf in range(halves)],
                            axis=1))
    vs = _swap_list_laneblock(ws)
    vs = _swap_list_sublane(vs)
    y_scr[:, lb * LANES:(lb + 1) * LANES] = jnp.stack(vs, axis=1).reshape(tm, LANES)


def _s5_glu(y_scr, za_ref, dskip_ref, wglu_ref, bglu_ref, glu_scr):
    yp = _gelu(y_scr[...] + dskip_ref[...] * za_ref[...])
    glu_scr[...] = (yp * _sigmoid(_dot(yp.astype(BF16), wglu_ref[...]) + bglu_ref[...])).astype(BF16)


def _merge_ffn_kernel(x_ref, yvec0_ref, za0_ref, yvecn_ref, zan_ref, dskip_ref, sgu_ref, attn_ref,
                      gates_ref, wglu_ref, bglu_ref, wa_ref, wb_ref, wc_ref, wout_ref, gffn_ref,
                      wgate_ref, wup_ref, wdown_ref, gfin_ref, out_ref,
                      y_scr, glu_scr, mrg_scr, x1_scr, x2_scr, hf_scr, act_scr, *, final_norm):
    tm, d = x_ref.shape
    width = za0_ref.shape[-1]
    d_ff = wdown_ref.shape[0]
    n_lane_blocks = width // LANES
    step = pl.program_id(0)
    n_tiles = pl.num_programs(0) - 1

    def write_out():
        x2 = x2_scr[...]
        out_ref[...] = _rms(x2, gfin_ref[...]) if final_norm else x2

    @pl.when(step == 0)
    def _():
        x2_scr[...] = jnp.zeros_like(x2_scr)
        for lb in range(n_lane_blocks):
            _s5_regroup_block(yvec0_ref, y_scr, lb)
        _s5_glu(y_scr, za0_ref, dskip_ref, wglu_ref, bglu_ref, glu_scr)

    @pl.when(step == n_tiles)
    def _():
        write_out()

    @pl.when(step < n_tiles)
    def _():
        side_jobs = [write_out]
        side_jobs += [functools.partial(_s5_regroup_block, yvecn_ref, y_scr, lb)
                      for lb in range(n_lane_blocks)]
        side_jobs += [functools.partial(_s5_glu, y_scr, zan_ref, dskip_ref, wglu_ref, bglu_ref,
                                        glu_scr)]

        for c in range(d // width):
            lanes = slice(c * width, (c + 1) * width)
            g1 = gates_ref[:, d + c * width:d + (c + 1) * width].astype(F32)
            g2 = gates_ref[:, 2 * d + c * width:2 * d + (c + 1) * width].astype(F32)
            x1_scr[:, lanes] = (g1 * _dot(sgu_ref[...], wb_ref[:, lanes])
                                + g2 * _dot(attn_ref[...], wc_ref[:, lanes]))
        for c in range(d // width):
            lanes = slice(c * width, (c + 1) * width)
            g0 = gates_ref[:, c * width:(c + 1) * width].astype(F32)
            mrg_scr[:, lanes] = (g0 * _dot(glu_scr[...], wa_ref[:, lanes])
                                 + x1_scr[:, lanes]).astype(BF16)
        x1_scr[...] = x_ref[...] + _dot(mrg_scr[...], wout_ref[...])

        hf_scr[...] = _rms(x1_scr[...], gffn_ref[...]).astype(BF16)
        n_chunks = d_ff // FFN_CHUNK
        assert len(side_jobs) <= n_chunks
        for c in range(n_chunks):
            cols = slice(c * FFN_CHUNK, (c + 1) * FFN_CHUNK)
            gate = _dot(hf_scr[...], wgate_ref[:, cols])
            up = _dot(hf_scr[...], wup_ref[:, cols])
            act_scr[:, cols] = (gate * _sigmoid(gate) * up).astype(BF16)
            if c < len(side_jobs):
                side_jobs[c]()
        x2_scr[...] = x1_scr[...] + _dot(act_scr[...], wdown_ref[...])


def _merge_ffn(x, yvec, za, d_skip, sgu, attn, gates, w_glu, b_glu, w_a, w_b, w_c, w_out, g_ffn,
               w_gate, w_up, w_down, g_final, final_norm):
    n_tok, d = x.shape
    width = za.shape[-1]
    d_ff = w_down.shape[0]
    tm = TOKEN_TILE
    n_groups = yvec.shape[0]
    n_tiles = n_tok // tm
    last = n_tiles - 1
    tile = lambda w: pl.BlockSpec((tm, w), lambda i: (jnp.minimum(i, last), 0))
    tile_next = lambda w: pl.BlockSpec((tm, w), lambda i: (jnp.minimum(i + 1, last), 0))
    yvec_block = (n_groups, tm // S5_CHUNK, yvec.shape[-1])
    return pl.pallas_call(
        functools.partial(_merge_ffn_kernel, final_norm=final_norm),
        out_shape=jax.ShapeDtypeStruct((n_tok, d), F32),
        grid=(n_tiles + 1,),
        in_specs=[
            tile(d),
            pl.BlockSpec(yvec_block, lambda i: (0, 0, 0), pipeline_mode=pl.Buffered(1)),
            pl.BlockSpec((tm, width), lambda i: (0, 0), pipeline_mode=pl.Buffered(1)),
            pl.BlockSpec(yvec_block, lambda i: (0, jnp.minimum(i + 1, last), 0)),
            tile_next(width), _resident((1, width)),
            tile(width), tile(width), tile(gates.shape[-1]),
            _resident(w_glu.shape), _resident((1, width)),
            _resident(w_a.shape), _resident(w_b.shape), _resident(w_c.shape),
            _resident(w_out.shape), _resident((1, d)),
            _resident(w_gate.shape), _resident(w_up.shape), _resident(w_down.shape),
            _resident((1, d)),
        ],
        out_specs=pl.BlockSpec((tm, d), lambda i: (jnp.maximum(i - 1, 0), 0)),
        scratch_shapes=[
            pltpu.VMEM((tm, width), F32),
            pltpu.VMEM((tm, width), BF16),
            pltpu.VMEM((tm, d), BF16),
            pltpu.VMEM((tm, d), F32),
            pltpu.VMEM((tm, d), F32),
            pltpu.VMEM((tm, d), BF16),
            pltpu.VMEM((tm, d_ff), BF16),
        ],
        compiler_params=_params(1),
        name="merge_ffn",
    )(x, yvec, za, yvec, za, d_skip, sgu, attn, gates, w_glu, b_glu, w_a, w_b, w_c, w_out, g_ffn,
      w_gate, w_up, w_down, g_final)


def kernel(x, mem, g_mix, w_in, b_gate, a_re, a_im, log_dt, bm_re, bm_im, cm_re, cm_im, d_skip, w_glu, b_glu, g_sgu, beta_sgu, w_s, b_s, g_mem, w_kv, w_a_out, w_b_out, w_c_out, w_out, g_ffn, w_gate, w_up, w_down, g_final):
    bsz, seq, d = x.shape
    depth = w_in.shape[0]
    assert bsz == SUBLANES
    for tile_rows in (INPROJ_TILE, TOKEN_TILE):
        assert seq % tile_rows == 0 and tile_rows % (SUBLANES * S5_CHUNK) == 0
    row = lambda v: v.reshape(1, -1)
    flat = lambda t: t.reshape(bsz * seq, t.shape[-1])
    for l in range(depth):
        swap = lambda w: jnp.swapaxes(w, 1, 2)
        k, v, w_in_b, *s5_ops = _mem_kv(
            mem, row(g_mem[l]), w_kv[l], w_in[l], a_re[l][:, None, :], a_im[l][:, None, :],
            log_dt[l][:, None, None], swap(bm_re[l]), swap(bm_im[l]), cm_re[l], cm_im[l])
        (za, uvec, sgu, attn, gates, w_glu_b, w_a_b, w_b_b, w_c_b, w_out_b, w_gate_b, w_up_b,
         w_down_b) = _inproj(
            x, row(g_mix[l]), w_in_b, row(b_gate[l]), row(g_sgu[l]),
            row(beta_sgu[l]), w_s[l], b_s[l].T, k, v,
            [w_glu[l], w_a_out[l], w_b_out[l], w_c_out[l], w_out[l], w_gate[l], w_up[l], w_down[l]])
        yvec = _s5_chunk(uvec, *s5_ops)
        x = _merge_ffn(
            flat(x), yvec, flat(za), row(d_skip[l]), flat(sgu), flat(attn), flat(gates),
            w_glu_b, row(b_glu[l]), w_a_b, w_b_b, w_c_b, w_out_b,
            row(g_ffn[l]), w_gate_b, w_up_b, w_down_b, row(g_final),
            l == depth - 1).reshape(bsz, seq, d)
    return x
```
